```python
import jax
import jax.numpy as jnp
from jax import lax
import numpy as np


D_MODEL = 2048
BATCH = 2
SEQ = 8192
DEPTH = 4

GRID_W = 64
CTX_LEN = 256

HEAD_DIM = 128
N_HEADS = D_MODEL // HEAD_DIM
N_KV_HEADS = N_HEADS // 4
GQA_GROUP = N_HEADS // N_KV_HEADS
ATTN_DIM = N_HEADS * HEAD_DIM
KV_DIM = N_KV_HEADS * HEAD_DIM
Q_BLOCK = 128
ATTN_SCALE = HEAD_DIM ** -0.5
ROPE_THETA = 10000.0
AXIS_DIM = HEAD_DIM // 2
ODD_IN = 2 * ATTN_DIM + 2 * KV_DIM

WIDTH_A = D_MODEL
WIDTH_B = D_MODEL
CONV_A_WIDTH = 31
CONV_B_WIDTH = 3
EVEN_IN = 3 * WIDTH_A + 4 * WIDTH_B
EVEN_SPLITS = [WIDTH_A, 2 * WIDTH_A, 3 * WIDTH_A, 3 * WIDTH_A + WIDTH_B,
               3 * WIDTH_A + 2 * WIDTH_B, 3 * WIDTH_A + 3 * WIDTH_B]

N_EVEN = (DEPTH + 1) // 2
N_ODD = DEPTH // 2
DEEPNORM_ALPHA = (2.0 * DEPTH) ** 0.25
DEEPNORM_BETA = (8.0 * DEPTH) ** -0.25
LN_EPS = 1e-5
RMS_EPS = 1e-6

kernel_name = 'hybrid_conv_gqa_prefix_dit_block'


def layer_norm(x, g, b):
    xf = x.astype(jnp.float32)
    mu = jnp.mean(xf, axis=-1, keepdims=True)
    var = jnp.mean(jnp.square(xf - mu), axis=-1, keepdims=True)
    return ((xf - mu) * lax.rsqrt(var + LN_EPS)).astype(x.dtype) * g + b


def rms_norm(x, g):
    xf = x.astype(jnp.float32)
    inv = lax.rsqrt(jnp.mean(jnp.square(xf), axis=-1, keepdims=True) + RMS_EPS)
    return (xf * inv).astype(x.dtype) * g


def adaln(cond, w, b):
    m = jax.nn.silu(cond) @ w + b
    return jnp.split(m, 3, axis=-1)


def depthwise_conv(u, w):
    pad = w.shape[0] // 2
    return lax.conv_general_dilated(u, w[:, None, :].astype(u.dtype), (1,), [(pad, pad)],
                                    dimension_numbers=('NWC', 'WIO', 'NWC'),
                                    feature_group_count=u.shape[-1])


def rope_tables(n_tokens, dtype):
    rows_n = n_tokens // GRID_W
    row = jnp.repeat(jnp.arange(rows_n, dtype=jnp.float32), GRID_W)
    col = jnp.tile(jnp.arange(GRID_W, dtype=jnp.float32), rows_n)
    inv_freq = ROPE_THETA ** (-jnp.arange(0, AXIS_DIM, 2, dtype=jnp.float32) / AXIS_DIM)
    ang_r = row[:, None] * inv_freq[None, :]
    ang_c = col[:, None] * inv_freq[None, :]
    return (jnp.cos(ang_r).astype(dtype), jnp.sin(ang_r).astype(dtype),
            jnp.cos(ang_c).astype(dtype), jnp.sin(ang_c).astype(dtype))


def rotate(xh, cos, sin):
    x1, x2 = jnp.split(xh, 2, axis=-1)
    cos = cos[None, :, None, :]
    sin = sin[None, :, None, :]
    return jnp.concatenate([x1 * cos - x2 * sin, x1 * sin + x2 * cos], axis=-1)


def apply_rope_2d(x, tabs):
    cos_r, sin_r, cos_c, sin_c = tabs
    x_row, x_col = jnp.split(x, 2, axis=-1)
    return jnp.concatenate([rotate(x_row, cos_r, sin_r), rotate(x_col, cos_c, sin_c)], axis=-1)


def gqa(q, k, v):
    b, nq = q.shape[:2]
    qg = q.reshape(b, nq, N_KV_HEADS, GQA_GROUP, HEAD_DIM)
    s = jnp.einsum('bqkgd,bnkd->bkgqn', qg, k, preferred_element_type=jnp.float32) * ATTN_SCALE
    p = jax.nn.softmax(s, axis=-1).astype(v.dtype)
    o = jnp.einsum('bkgqn,bnkd->bqkgd', p, v)
    return o.reshape(b, nq, ATTN_DIM)


def q_gate_proj(h, w_qg, qg):
    q, g = jnp.split(h @ w_qg, 2, axis=-1)
    q = rms_norm(q.reshape(h.shape[0], h.shape[1], N_HEADS, HEAD_DIM), qg)
    return q, g


def kv_proj(h, w_kv, kg):
    k, v = jnp.split(h @ w_kv, 2, axis=-1)
    k = rms_norm(k.reshape(h.shape[0], h.shape[1], N_KV_HEADS, HEAD_DIM), kg)
    v = v.reshape(h.shape[0], h.shape[1], N_KV_HEADS, HEAD_DIM)
    return k, v


def attn_mixer(h_lat, h_ctx, w_in, qg, kg, w_out, tabs, ctx_out):
    w_qg = w_in[:, :2 * ATTN_DIM]
    w_kv = w_in[:, 2 * ATTN_DIM:]
    b, s = h_lat.shape[:2]
    k_c, v_c = kv_proj(h_ctx, w_kv, kg)
    q_l, g_l = q_gate_proj(h_lat, w_qg, qg)
    k_l, v_l = kv_proj(h_lat, w_kv, kg)
    q_l = apply_rope_2d(q_l, tabs)
    k_l = apply_rope_2d(k_l, tabs)
    k_all = jnp.concatenate([k_c, k_l], axis=1)
    v_all = jnp.concatenate([v_c, v_l], axis=1)
    n_blocks = s // Q_BLOCK
    q_blocks = q_l.reshape(b, n_blocks, Q_BLOCK, N_HEADS, HEAD_DIM).swapaxes(0, 1)
    o = lax.map(lambda qb: gqa(qb, k_all, v_all), q_blocks)
    o = o.swapaxes(0, 1).reshape(b, s, ATTN_DIM)
    y_lat = (o * jax.nn.silu(g_l)) @ w_out
    if not ctx_out:
        return y_lat, None
    q_c, g_c = q_gate_proj(h_ctx, w_qg, qg)
    o_c = gqa(q_c, k_c, v_c)
    y_ctx = (o_c * jax.nn.silu(g_c)) @ w_out
    return y_lat, y_ctx


def conv_mixer(h, w_in, ca_w, ca_b, na_g, na_b, cb_w, w_out):
    a_val, a_glu, a_gate, b_x, b_b, b_c, b_gate = jnp.split(h @ w_in, EVEN_SPLITS, axis=-1)
    u = a_val * jax.nn.sigmoid(a_glu)
    u = depthwise_conv(u, ca_w) + ca_b
    u = jax.nn.silu(layer_norm(u, na_g, na_b))
    a_out = u * jax.nn.silu(a_gate)
    v = depthwise_conv(b_c * b_x, cb_w)
    b_out = b_b * v * jax.nn.silu(b_gate)
    return jnp.concatenate([a_out, b_out], axis=-1) @ w_out


def setup_inputs(seed: int = 0) -> dict:
    key = jax.random.key(seed)
    ks = jax.random.split(key, 19)

    def nrm(k, shape, s):
        return jax.random.normal(k, shape, jnp.float32) * s

    return {
        'x': nrm(ks[0], (BATCH, SEQ, D_MODEL), 1.0),
        'c': nrm(ks[1], (BATCH, D_MODEL), 1.0),
        'ctx': nrm(ks[2], (BATCH, CTX_LEN, D_MODEL), 1.0),
        'c_ctx': nrm(ks[3], (D_MODEL,), 1.0),
        'w_mod': nrm(ks[4], (DEPTH, D_MODEL, 3 * D_MODEL), D_MODEL ** -0.5),
        'b_mod': nrm(ks[5], (DEPTH, 3 * D_MODEL), 0.01),
        'post_ln_g': 1.0 + nrm(ks[6], (DEPTH, D_MODEL), 0.02),
        'post_ln_b': nrm(ks[7], (DEPTH, D_MODEL), 0.02),
        'w_in_e': nrm(ks[8], (N_EVEN, D_MODEL, EVEN_IN), D_MODEL ** -0.5),
        'conv_a_w': nrm(ks[9], (N_EVEN, CONV_A_WIDTH, WIDTH_A), CONV_A_WIDTH ** -0.5),
        'conv_a_b': nrm(ks[10], (N_EVEN, WIDTH_A), 0.02),
        'norm_a_g': 1.0 + nrm(ks[11], (N_EVEN, WIDTH_A), 0.02),
        'norm_a_b': nrm(ks[12], (N_EVEN, WIDTH_A), 0.02),
        'conv_b_w': nrm(ks[13], (N_EVEN, CONV_B_WIDTH, WIDTH_B), CONV_B_WIDTH ** -0.5),
        'w_out_e': nrm(ks[14], (N_EVEN, WIDTH_A + WIDTH_B, D_MODEL),
                       DEEPNORM_BETA * (WIDTH_A + WIDTH_B) ** -0.5),
        'w_in_o': nrm(ks[15], (N_ODD, D_MODEL, ODD_IN), D_MODEL ** -0.5),
        'q_norm_g': 1.0 + nrm(ks[16], (N_ODD, HEAD_DIM), 0.02),
        'k_norm_g': 1.0 + nrm(ks[17], (N_ODD, HEAD_DIM), 0.02),
        'w_out_o': nrm(ks[18], (N_ODD, ATTN_DIM, D_MODEL), DEEPNORM_BETA * ATTN_DIM ** -0.5),
    }


def reference(x, c, ctx, c_ctx, w_mod, b_mod, post_ln_g, post_ln_b, w_in_e, conv_a_w, conv_a_b,
              norm_a_g, norm_a_b, conv_b_w, w_out_e, w_in_o, q_norm_g, k_norm_g, w_out_o):
    tabs = rope_tables(x.shape[1], x.dtype)
    for layer in range(DEPTH):
        last = layer == DEPTH - 1
        is_attn = layer % 2 == 1
        i = layer // 2
        need_ctx = (not last) or is_attn
        shift, scale, gate = adaln(c, w_mod[layer], b_mod[layer])
        h_lat = x * (1 + scale[:, None, :]) + shift[:, None, :]
        if need_ctx:
            shift_c, scale_c, gate_c = adaln(c_ctx, w_mod[layer], b_mod[layer])
            h_ctx = ctx * (1 + scale_c) + shift_c
        if is_attn:
            y_lat, y_ctx = attn_mixer(h_lat, h_ctx, w_in_o[i], q_norm_g[i], k_norm_g[i],
                                      w_out_o[i], tabs, not last)
        else:
            y_lat = conv_mixer(h_lat, w_in_e[i], conv_a_w[i], conv_a_b[i], norm_a_g[i],
                               norm_a_b[i], conv_b_w[i], w_out_e[i])
            if not last:
                y_ctx = conv_mixer(h_ctx, w_in_e[i], conv_a_w[i], conv_a_b[i], norm_a_g[i],
                                   norm_a_b[i], conv_b_w[i], w_out_e[i])
        x = layer_norm(DEEPNORM_ALPHA * x + gate[:, None, :] * y_lat,
                       post_ln_g[layer], post_ln_b[layer])
        if not last:
            ctx = layer_norm(DEEPNORM_ALPHA * ctx + gate_c * y_ctx,
                             post_ln_g[layer], post_ln_b[layer])
    return x
```

```python
import functools
import math

import jax
import jax.numpy as jnp
from jax import lax
from jax.experimental import pallas as pl
from jax.experimental.pallas import tpu as pltpu

F32 = jnp.float32
BF16 = jnp.bfloat16

HEAD_DIM = 128
GQA_GROUP = 4
GRID_W = 64
ROPE_THETA = 10000.0
LN_EPS = 1e-5
RMS_EPS = 1e-6

LANES = 128
F32_SUBLANES = 8
BF16_SUBLANES = 16
VMEM_LIMIT_BYTES = 56 * 1024 * 1024

MOD_ROWS = 8
CONV_A_HALO = 16
CONV_B_HALO = 8


def _pick(n, cands):
    for c in cands:
        if n % c == 0:
            return c
    return n


def _sigmoid(x):
    return 1.0 / (1.0 + jnp.exp(-x))


def _silu(x):
    return x * _sigmoid(x)


def _cparams(*sem):
    return pltpu.CompilerParams(dimension_semantics=sem, vmem_limit_bytes=VMEM_LIMIT_BYTES)


def _adaln_kernel(c_ref, w_ref, b_ref, o_ref):
    s = _silu(c_ref[...]).astype(BF16)
    w = w_ref[...].astype(BF16)
    o_ref[...] = jnp.dot(s, w, preferred_element_type=F32) + b_ref[...]


def _adaln_all(c, c_ctx, w_mod, b_mod):
    n_layers, d, d3 = w_mod.shape
    nb = c.shape[0]
    cc = jnp.zeros((MOD_ROWS, d), F32).at[:nb].set(c).at[nb].set(c_ctx)
    tn = _pick(d3, (1024, 512, 256, 128))
    out = pl.pallas_call(
        _adaln_kernel,
        grid=(n_layers, d3 // tn),
        in_specs=[
            pl.BlockSpec((MOD_ROWS, d), lambda l, j: (0, 0)),
            pl.BlockSpec((None, d, tn), lambda l, j: (l, 0, j)),
            pl.BlockSpec((None, 1, tn), lambda l, j: (l, 0, j)),
        ],
        out_specs=pl.BlockSpec((None, MOD_ROWS, tn), lambda l, j: (l, 0, j)),
        out_shape=jax.ShapeDtypeStruct((n_layers, MOD_ROWS, d3), F32),
        compiler_params=_cparams("parallel", "parallel"),
        name="adaln_mod",
    )(cc, w_mod, b_mod.reshape(n_layers, 1, d3))
    return out.reshape(n_layers, MOD_ROWS, 3, d).transpose(0, 2, 1, 3).reshape(n_layers, 3, MOD_ROWS, 1, d)


def _mod_spec(layer, chunk, row_of, d):
    return pl.BlockSpec((None, None, None, 1, d), lambda b, *_: (layer, chunk, row_of(b), 0, 0))


def _even_inproj_kernel(x_ref, sh_ref, sc_ref, w_aval, w_aglu, w_agate, w_bx, w_bb, w_bc, w_bgate,
                        u_ref, ga_ref, w_ref, gb_ref, h_scr):
    @pl.when(pl.program_id(2) == 0)
    def _():
        h_scr[...] = (x_ref[...] * (1.0 + sc_ref[...]) + sh_ref[...]).astype(BF16)

    h = h_scr[...]

    def proj(w):
        return jnp.dot(h, w[...], preferred_element_type=F32)

    u_ref[...] = proj(w_aval) * _sigmoid(proj(w_aglu))
    ga_ref[...] = _silu(proj(w_agate))
    w_ref[...] = proj(w_bc) * proj(w_bx)
    gb_ref[...] = proj(w_bb) * _silu(proj(w_bgate))


def _even_inproj(x, mod5, layer, row_of, w_in, li):
    nb, t, d = x.shape
    width = w_in.shape[2] // 7
    tm = _pick(t, (512, 256, 128))
    tn = _pick(width, (256, 128))
    ncol = width // tn
    w_specs = [pl.BlockSpec((None, d, tn), lambda b, i, j, c=c: (li, 0, c * ncol + j)) for c in range(7)]
    out_spec = pl.BlockSpec((None, tm, tn), lambda b, i, j: (b, i, j))
    out_sds = jax.ShapeDtypeStruct((nb, t, width), F32)
    return pl.pallas_call(
        _even_inproj_kernel,
        grid=(nb, t // tm, ncol),
        in_specs=[pl.BlockSpec((None, tm, d), lambda b, i, j: (b, i, 0)),
                  _mod_spec(layer, 0, row_of, d), _mod_spec(layer, 1, row_of, d)] + w_specs,
        out_specs=[out_spec] * 4,
        out_shape=[out_sds] * 4,
        scratch_shapes=[pltpu.VMEM((tm, d), BF16)],
        compiler_params=_cparams("parallel", "parallel", "arbitrary"),
        name="even_inproj",
    )(x, mod5, mod5, *([w_in] * 7))


def _even_conv_kernel(u_ref, up_ref, un_ref, w_ref, wp_ref, wn_ref, ga_ref, gb_ref,
                      caw_ref, cab_ref, ng_ref, nbias_ref, cbw_ref, z_ref,
                      ubuf, wbuf, cbuf, *, tt, width, n_tiles, ka, kb, row_chunk, ln_chunk):
    i = pl.program_id(1)
    has_prev = i > 0
    has_next = i < n_tiles - 1
    n_lane_tiles = width // LANES
    for c in range(n_lane_tiles):
        cs = slice(c * LANES, (c + 1) * LANES)
        ubuf[c, 0:CONV_A_HALO, :] = jnp.where(has_prev, up_ref[:, cs], 0.0)
        ubuf[c, CONV_A_HALO:CONV_A_HALO + tt, :] = u_ref[:, cs]
        ubuf[c, CONV_A_HALO + tt:, :] = jnp.where(has_next, un_ref[:, cs], 0.0)
        wbuf[c, 0:CONV_B_HALO, :] = jnp.where(has_prev, wp_ref[:, cs], 0.0)
        wbuf[c, CONV_B_HALO:CONV_B_HALO + tt, :] = w_ref[:, cs]
        wbuf[c, CONV_B_HALO + tt:, :] = jnp.where(has_next, wn_ref[:, cs], 0.0)

    off_a = CONV_A_HALO - ka // 2
    off_b = CONV_B_HALO - kb // 2

    for c in range(n_lane_tiles):
        cs = slice(c * LANES, (c + 1) * LANES)

        def conv_rows(r, carry, c=c, cs=cs):
            r0 = pl.multiple_of(r * row_chunk, row_chunk)
            bias = jnp.broadcast_to(cab_ref[:, cs], (F32_SUBLANES, LANES))
            accs = [bias] * (row_chunk // F32_SUBLANES)
            for k in range(ka):
                wk = caw_ref[k:k + 1, cs]
                accs = [a + ubuf[c, pl.ds(r0 + (off_a + k + F32_SUBLANES * s), F32_SUBLANES), :] * wk
                        for s, a in enumerate(accs)]
            for s, a in enumerate(accs):
                cbuf[pl.ds(r0 + F32_SUBLANES * s, F32_SUBLANES), cs] = a
            return carry

        lax.fori_loop(0, tt // row_chunk, conv_rows, 0)

    def norm_rows(r, carry):
        r0 = pl.multiple_of(r * ln_chunk, ln_chunk)
        rows = pl.ds(r0, ln_chunk)
        xc = cbuf[rows, :]
        mu = jnp.mean(xc, axis=-1, keepdims=True)
        dev = xc - mu
        var = jnp.mean(dev * dev, axis=-1, keepdims=True)
        y = dev * lax.rsqrt(var + LN_EPS) * ng_ref[...] + nbias_ref[...]
        z_ref[rows, 0:width] = (_silu(y) * ga_ref[rows, :]).astype(BF16)
        for c in range(n_lane_tiles):
            cs = slice(c * LANES, (c + 1) * LANES)
            parts = []
            for s in range(ln_chunk // F32_SUBLANES):
                v = None
                for k in range(kb):
                    term = (wbuf[c, pl.ds(r0 + (off_b + k + F32_SUBLANES * s), F32_SUBLANES), :]
                            * cbw_ref[k:k + 1, cs])
                    v = term if v is None else v + term
                parts.append(v)
            v = jnp.concatenate(parts, axis=0)
            z_ref[rows, width + c * LANES:width + (c + 1) * LANES] = (gb_ref[rows, cs] * v).astype(BF16)
        return carry

    lax.fori_loop(0, tt // ln_chunk, norm_rows, 0, unroll=2)


def _even_conv(u, ga, w, gb, caw, cab, ng, nbias, cbw):
    nb, t, width = u.shape
    ka, kb = caw.shape[0], cbw.shape[0]
    assert ka // 2 <= CONV_A_HALO and kb // 2 <= CONV_B_HALO
    tt = _pick(t, (256, 128))
    n_tiles = t // tt
    ha, hb = tt // CONV_A_HALO, tt // CONV_B_HALO

    def cur(b, i):
        return (b, i, 0)

    def halo_specs(halo, per_tile):
        last = t // halo - 1
        return [pl.BlockSpec((None, halo, width), lambda b, i: (b, jnp.maximum(i * per_tile - 1, 0), 0)),
                pl.BlockSpec((None, halo, width), lambda b, i: (b, jnp.minimum((i + 1) * per_tile, last), 0))]

    tile = pl.BlockSpec((None, tt, width), cur)
    vec = lambda rows: pl.BlockSpec((rows, width), lambda b, i: (0, 0))
    kern = functools.partial(_even_conv_kernel, tt=tt, width=width, n_tiles=n_tiles, ka=ka, kb=kb,
                             row_chunk=32, ln_chunk=BF16_SUBLANES)
    return pl.pallas_call(
        kern,
        grid=(nb, n_tiles),
        in_specs=[tile] + halo_specs(CONV_A_HALO, ha) + [tile] + halo_specs(CONV_B_HALO, hb) + [tile, tile]
                 + [vec(ka), vec(1), vec(1), vec(1), vec(kb)],
        out_specs=pl.BlockSpec((None, tt, 2 * width), cur),
        out_shape=jax.ShapeDtypeStruct((nb, t, 2 * width), BF16),
        scratch_shapes=[pltpu.VMEM((width // LANES, tt + 2 * CONV_A_HALO, LANES), F32),
                        pltpu.VMEM((width // LANES, tt + 2 * CONV_B_HALO, LANES), F32),
                        pltpu.VMEM((tt, width), F32)],
        compiler_params=_cparams("parallel", "parallel"),
        name="even_conv",
    )(u, u, u, w, w, w, ga, gb, caw, cab.reshape(1, width), ng.reshape(1, width), nbias.reshape(1, width), cbw)


def _outproj_ln_kernel(z_ref, w_ref, x_ref, gate_ref, g_ref, b_ref, o_ref, *, alpha):
    y = jnp.dot(z_ref[...], w_ref[...], preferred_element_type=F32)
    r = alpha * x_ref[...] + gate_ref[...] * y
    mu = jnp.mean(r, axis=-1, keepdims=True)
    dev = r - mu
    var = jnp.mean(dev * dev, axis=-1, keepdims=True)
    o_ref[...] = dev * lax.rsqrt(var + LN_EPS) * g_ref[...] + b_ref[...]


def _outproj_ln(z, w_out, li, x, mod5, layer, row_of, ln_g, ln_b, alpha):
    nb, t, d = x.shape
    kz = z.shape[-1]
    tm = _pick(t, (512, 256, 128))
    tile = lambda b, i: (b, i, 0)
    const = lambda b, i: (0, 0)
    return pl.pallas_call(
        functools.partial(_outproj_ln_kernel, alpha=alpha),
        grid=(nb, t // tm),
        in_specs=[pl.BlockSpec((None, tm, kz), tile),
                  pl.BlockSpec((None, kz, d), lambda b, i: (li, 0, 0), pipeline_mode=pl.Buffered(1)),
                  pl.BlockSpec((None, tm, d), tile),
                  _mod_spec(layer, 2, row_of, d),
                  pl.BlockSpec((1, d), const), pl.BlockSpec((1, d), const)],
        out_specs=pl.BlockSpec((None, tm, d), tile),
        out_shape=jax.ShapeDtypeStruct((nb, t, d), F32),
        compiler_params=_cparams("parallel", "parallel"),
        name="outproj_ln",
    )(z, w_out, x, mod5, ln_g.reshape(1, d), ln_b.reshape(1, d))


def _odd_inproj_kernel(*refs, rope, need_q, q_scale):
    x_ref, sh_ref, sc_ref = refs[:3]
    refs = refs[3:]
    if need_q:
        wq_ref, wg_ref = refs[:2]
        refs = refs[2:]
    wk_ref, wv_ref = refs[:2]
    refs = refs[2:]
    if need_q:
        qg_ref = refs[0]
        refs = refs[1:]
    kg_ref = refs[0]
    refs = refs[1:]
    if rope:
        cos_ref, sa_ref, sb_ref = refs[:3]
        refs = refs[3:]
    if need_q:
        q_ref, sg_ref = refs[:2]
        refs = refs[2:]
    k_ref, vt_ref, h_scr = refs

    @pl.when(pl.program_id(2) == 0)
    def _():
        h_scr[...] = (x_ref[...] * (1.0 + sc_ref[...]) + sh_ref[...]).astype(BF16)

    h = h_scr[...]

    def proj(w):
        return jnp.dot(h, w[...], preferred_element_type=F32)

    def norm_rope(t, g):
        ms = jnp.mean(t * t, axis=-1, keepdims=True)
        t = t * lax.rsqrt(ms + RMS_EPS) * g
        if rope:
            t = (t * cos_ref[...] + pltpu.roll(t, HEAD_DIM - HEAD_DIM // 4, 1) * sa_ref[...]
                 + pltpu.roll(t, HEAD_DIM // 4, 1) * sb_ref[...])
        return t

    k_ref[...] = norm_rope(proj(wk_ref), kg_ref[...]).astype(BF16)
    v = proj(wv_ref)
    for s in range(v.shape[0] // LANES):
        vt_ref[s] = v[s * LANES:(s + 1) * LANES, :].T.astype(BF16)
    if need_q:
        q = proj(wq_ref)
        for hh in range(GQA_GROUP):
            hs = slice(hh * HEAD_DIM, (hh + 1) * HEAD_DIM)
            q_ref[:, hs] = (norm_rope(q[:, hs], qg_ref[...]) * q_scale).astype(BF16)
        sg_ref[...] = _silu(proj(wg_ref))


def _odd_inproj(x, mod5, layer, row_of, w_in, li, q_gain, k_gain, rope_tabs, need_q, q_scale):
    nb, t, d = x.shape
    kvh = w_in.shape[2] // (2 * GQA_GROUP * HEAD_DIM + 2 * HEAD_DIM)
    dq = kvh * GQA_GROUP * HEAD_DIM
    dkv = kvh * HEAD_DIM
    gw = GQA_GROUP * HEAD_DIM
    tm = _pick(t, (512, 256, 128))
    rope = rope_tabs is not None
    idx = lambda b, i, j: (b, i, 0)
    in_specs = [pl.BlockSpec((None, tm, d), idx), _mod_spec(layer, 0, row_of, d), _mod_spec(layer, 1, row_of, d)]
    args = [x, mod5, mod5]
    if need_q:
        in_specs += [pl.BlockSpec((None, d, gw), lambda b, i, j: (li, 0, j)),
                     pl.BlockSpec((None, d, gw), lambda b, i, j: (li, 0, kvh + j))]
        args += [w_in, w_in]
    koff, voff = 2 * dq // HEAD_DIM, 2 * dq // HEAD_DIM + kvh
    in_specs += [pl.BlockSpec((None, d, HEAD_DIM), lambda b, i, j: (li, 0, koff + j)),
                 pl.BlockSpec((None, d, HEAD_DIM), lambda b, i, j: (li, 0, voff + j))]
    args += [w_in, w_in]
    gain_spec = pl.BlockSpec((1, HEAD_DIM), lambda b, i, j: (0, 0))
    if need_q:
        in_specs.append(gain_spec)
        args.append(q_gain.reshape(1, HEAD_DIM))
    in_specs.append(gain_spec)
    args.append(k_gain.reshape(1, HEAD_DIM))
    if rope:
        in_specs += [pl.BlockSpec((tm, HEAD_DIM), lambda b, i, j: (i, 0))] * 3
        args += list(rope_tabs)
    out_specs, out_shape = [], []
    if need_q:
        out_specs += [pl.BlockSpec((None, tm, gw), lambda b, i, j: (b, i, j))] * 2
        out_shape += [jax.ShapeDtypeStruct((nb, t, dq), BF16), jax.ShapeDtypeStruct((nb, t, dq), F32)]
    out_specs += [pl.BlockSpec((None, tm, HEAD_DIM), lambda b, i, j: (b, i, j)),
                  pl.BlockSpec((None, None, tm // LANES, HEAD_DIM, LANES), lambda b, i, j: (b, j, i, 0, 0))]
    out_shape += [jax.ShapeDtypeStruct((nb, t, dkv), BF16),
                  jax.ShapeDtypeStruct((nb, kvh, t // LANES, HEAD_DIM, LANES), BF16)]
    return pl.pallas_call(
        functools.partial(_odd_inproj_kernel, rope=rope, need_q=need_q, q_scale=q_scale),
        grid=(nb, t // tm, kvh),
        in_specs=in_specs,
        out_specs=out_specs,
        out_shape=out_shape,
        scratch_shapes=[pltpu.VMEM((tm, d), BF16)],
        compiler_params=_cparams("parallel", "parallel", "arbitrary"),
        name="odd_inproj",
    )(*args)


def _flash_kernel(q_ref, k_ref, vt_ref, sg_ref, o_ref, acc, m_scr, l_scr, *, tq, tk, n_chunks):
    slabs = tk // LANES
    qs = jnp.concatenate([q_ref[:, hh * HEAD_DIM:(hh + 1) * HEAD_DIM] for hh in range(GQA_GROUP)], axis=0)
    m_scr[...] = jnp.full(m_scr.shape, -jnp.inf, F32)
    l_scr[...] = jnp.zeros(l_scr.shape, F32)
    acc[...] = jnp.zeros(acc.shape, F32)

    def chunk(c, carry):
        kc = k_ref[pl.ds(pl.multiple_of(c * tk, tk), tk), :]
        st = lax.dot_general(kc, qs, (((1,), (1,)), ((), ())), preferred_element_type=F32)
        m_old = m_scr[...]
        m_new = jnp.maximum(m_old, jnp.max(st, axis=0, keepdims=True))
        p = jnp.exp2(st - m_new)
        alpha = jnp.exp2(m_old - m_new)
        l_scr[...] = alpha * l_scr[...] + jnp.sum(p, axis=0, keepdims=True)
        m_scr[...] = m_new
        vt = jnp.concatenate([vt_ref[c * slabs + s] for s in range(slabs)], axis=1)
        acc[...] = alpha * acc[...] + jnp.dot(vt, p.astype(BF16), preferred_element_type=F32)
        return carry

    lax.fori_loop(0, n_chunks, chunk, 0)
    o = (acc[...] * (1.0 / l_scr[...])).T
    for hh in range(GQA_GROUP):
        hs = slice(hh * HEAD_DIM, (hh + 1) * HEAD_DIM)
        o_ref[:, hs] = (o[hh * tq:(hh + 1) * tq, :] * sg_ref[:, hs]).astype(BF16)


def _flash(q, k, vt, sg):
    nb, t, dq = q.shape
    n = k.shape[1]
    kvh = vt.shape[1]
    gw = GQA_GROUP * HEAD_DIM
    tq = _pick(t, (256, 128))
    tk = _pick(n, (768, 512, 256, 128))
    qspec = pl.BlockSpec((None, tq, gw), lambda b, g, i: (b, i, g))
    return pl.pallas_call(
        functools.partial(_flash_kernel, tq=tq, tk=tk, n_chunks=n // tk),
        grid=(nb, kvh, t // tq),
        in_specs=[qspec,
                  pl.BlockSpec((None, n, HEAD_DIM), lambda b, g, i: (b, 0, g)),
                  pl.BlockSpec((None, None, n // LANES, HEAD_DIM, LANES), lambda b, g, i: (b, g, 0, 0, 0)),
                  qspec],
        out_specs=qspec,
        out_shape=jax.ShapeDtypeStruct((nb, t, dq), BF16),
        scratch_shapes=[pltpu.VMEM((HEAD_DIM, GQA_GROUP * tq), F32),
                        pltpu.VMEM((1, GQA_GROUP * tq), F32),
                        pltpu.VMEM((1, GQA_GROUP * tq), F32)],
        compiler_params=_cparams("parallel", "parallel", "parallel"),
        name="flash_gqa",
    )(q, k, vt, sg)


def _rope_tables(n_tokens):
    quarter = HEAD_DIM // 4
    pos = jnp.arange(n_tokens, dtype=jnp.int32)
    row = (pos // GRID_W).astype(F32)
    col = (pos % GRID_W).astype(F32)
    inv_freq = ROPE_THETA ** (-jnp.arange(0, HEAD_DIM // 2, 2, dtype=F32) / (HEAD_DIM // 2))
    ang_r = row[:, None] * inv_freq[None, :]
    ang_c = col[:, None] * inv_freq[None, :]
    cr, sr, cc, sc = jnp.cos(ang_r), jnp.sin(ang_r), jnp.cos(ang_c), jnp.sin(ang_c)
    zero = jnp.zeros((n_tokens, quarter), F32)
    cos = jnp.concatenate([cr, cr, cc, cc], axis=-1)
    sin_a = jnp.concatenate([-sr, zero, -sc, zero], axis=-1)
    sin_b = jnp.concatenate([zero, sr, zero, sc], axis=-1)
    return cos, sin_a, sin_b


@jax.jit
def kernel(x, c, ctx, c_ctx, w_mod, b_mod, post_ln_g, post_ln_b, w_in_e, conv_a_w, conv_a_b, norm_a_g,
           norm_a_b, conv_b_w, w_out_e, w_in_o, q_norm_g, k_norm_g, w_out_o):
    nb, seq, _ = x.shape
    depth = w_mod.shape[0]
    alpha = (2.0 * depth) ** 0.25
    q_scale = HEAD_DIM ** -0.5 * math.log2(math.e)
    lat_row = lambda b: b
    ctx_row = lambda b: nb

    mod5 = _adaln_all(c, c_ctx, w_mod, b_mod)
    w_in_e, w_out_e, w_in_o, w_out_o = (w.astype(BF16) for w in (w_in_e, w_out_e, w_in_o, w_out_o))
    rope_tabs = _rope_tables(seq)

    for layer in range(depth):
        last = layer == depth - 1
        i = layer // 2
        ln_g, ln_b = post_ln_g[layer], post_ln_b[layer]
        if layer % 2 == 0:
            def conv_mix(s, row_of):
                u, ga, w, gb = _even_inproj(s, mod5, layer, row_of, w_in_e, i)
                z = _even_conv(u, ga, w, gb, conv_a_w[i], conv_a_b[i], norm_a_g[i], norm_a_b[i], conv_b_w[i])
                return _outproj_ln(z, w_out_e, i, s, mod5, layer, row_of, ln_g, ln_b, alpha)

            x_new = conv_mix(x, lat_row)
            if not last:
                ctx = conv_mix(ctx, ctx_row)
            x = x_new
        else:
            q_l, sg_l, k_l, vt_l = _odd_inproj(x, mod5, layer, lat_row, w_in_o, i, q_norm_g[i], k_norm_g[i],
                                               rope_tabs, True, q_scale)
            ctx_out = _odd_inproj(ctx, mod5, layer, ctx_row, w_in_o, i, q_norm_g[i], k_norm_g[i],
                                  None, not last, q_scale)
            k_c, vt_c = ctx_out[-2:]
            k_all = jnp.concatenate([k_c, k_l], axis=1)
            vt_all = jnp.concatenate([vt_c, vt_l], axis=2)
            z_l = _flash(q_l, k_all, vt_all, sg_l)
            x = _outproj_ln(z_l, w_out_o, i, x, mod5, layer, lat_row, ln_g, ln_b, alpha)
            if not last:
                q_c, sg_c = ctx_out[:2]
                z_c = _flash(q_c, k_c, vt_c, sg_c)
                ctx = _outproj_ln(z_c, w_out_o, i, ctx, mod5, layer, ctx_row, ln_g, ln_b, alpha)
    return x
```

```python
import functools
import math

import jax
import jax.numpy as jnp
from jax import lax
from jax.experimental import pallas as pl
from jax.experimental.pallas import tpu as pltpu

F32 = jnp.float32
BF16 = jnp.bfloat16

HEAD_DIM = 128
GQA_GROUP = 4
GRID_W = 64
ROPE_THETA = 10000.0
LN_EPS = 1e-5
RMS_EPS = 1e-6

LANES = 128
F32_SUBLANES = 8
BF16_SUBLANES = 16
VMEM_LIMIT_BYTES = 56 * 1024 * 1024

MOD_ROWS = 8
CONV_A_HALO = 16
CONV_B_HALO = 8


def _pick(n, cands):
    for c in cands:
        if n % c == 0:
            return c
    return n


def _sigmoid(x):
    return 1.0 / (1.0 + jnp.exp(-x))


def _silu(x):
    return x * _sigmoid(x)


def _cparams(*sem):
    return pltpu.CompilerParams(dimension_semantics=sem, vmem_limit_bytes=VMEM_LIMIT_BYTES)


def _adaln_kernel(c_ref, w_ref, b_ref, o_ref):
    s = _silu(c_ref[...]).astype(BF16)
    w = w_ref[...].astype(BF16)
    o_ref[...] = jnp.dot(s, w, preferred_element_type=F32) + b_ref[...]


def _adaln_all(c, c_ctx, w_mod, b_mod):
    n_layers, d, d3 = w_mod.shape
    nb = c.shape[0]
    cc = jnp.zeros((MOD_ROWS, d), F32).at[:nb].set(c).at[nb].set(c_ctx)
    tn = _pick(d3, (1024, 512, 256, 128))
    out = pl.pallas_call(
        _adaln_kernel,
        grid=(n_layers, d3 // tn),
        in_specs=[
            pl.BlockSpec((MOD_ROWS, d), lambda l, j: (0, 0)),
            pl.BlockSpec((None, d, tn), lambda l, j: (l, 0, j)),
            pl.BlockSpec((None, 1, tn), lambda l, j: (l, 0, j)),
        ],
        out_specs=pl.BlockSpec((None, MOD_ROWS, tn), lambda l, j: (l, 0, j)),
        out_shape=jax.ShapeDtypeStruct((n_layers, MOD_ROWS, d3), F32),
        compiler_params=_cparams("parallel", "parallel"),
        name="adaln_mod",
    )(cc, w_mod, b_mod.reshape(n_layers, 1, d3))
    return out.reshape(n_layers, MOD_ROWS, 3, d).transpose(0, 2, 1, 3).reshape(n_layers, 3, MOD_ROWS, 1, d)


def _mod_spec(layer, chunk, row_of, d):
    return pl.BlockSpec((None, None, None, 1, d), lambda b, *_: (layer, chunk, row_of(b), 0, 0))


def _even_inproj_kernel(x_ref, sh_ref, sc_ref, w_aval, w_aglu, w_agate, w_bx, w_bb, w_bc, w_bgate,
                        u_ref, ga_ref, w_ref, gb_ref, h_scr):
    @pl.when(pl.program_id(2) == 0)
    def _():
        h_scr[...] = (x_ref[...] * (1.0 + sc_ref[...]) + sh_ref[...]).astype(BF16)

    h = h_scr[...]

    def proj(w):
        return jnp.dot(h, w[...], preferred_element_type=F32)

    u_ref[...] = proj(w_aval) * _sigmoid(proj(w_aglu))
    ga_ref[...] = _silu(proj(w_agate))
    w_ref[...] = proj(w_bc) * proj(w_bx)
    gb_ref[...] = proj(w_bb) * _silu(proj(w_bgate))


def _even_inproj(x, mod5, layer, row_of, w_in, li):
    nb, t, d = x.shape
    width = w_in.shape[2] // 7
    tm = _pick(t, (512, 256, 128))
    tn = _pick(width, (256, 128))
    ncol = width // tn
    w_specs = [pl.BlockSpec((None, d, tn), lambda b, i, j, c=c: (li, 0, c * ncol + j)) for c in range(7)]
    out_spec = pl.BlockSpec((None, tm, tn), lambda b, i, j: (b, i, j))
    out_sds = jax.ShapeDtypeStruct((nb, t, width), F32)
    return pl.pallas_call(
        _even_inproj_kernel,
        grid=(nb, t // tm, ncol),
        in_specs=[pl.BlockSpec((None, tm, d), lambda b, i, j: (b, i, 0)),
                  _mod_spec(layer, 0, row_of, d), _mod_spec(layer, 1, row_of, d)] + w_specs,
        out_specs=[out_spec] * 4,
        out_shape=[out_sds] * 4,
        scratch_shapes=[pltpu.VMEM((tm, d), BF16)],
        compiler_params=_cparams("parallel", "parallel", "arbitrary"),
        name="even_inproj",
    )(x, mod5, mod5, *([w_in] * 7))


def _even_conv_kernel(u_ref, up_ref, un_ref, w_ref, wp_ref, wn_ref, ga_ref, gb_ref,
                      caw_ref, cab_ref, ng_ref, nbias_ref, cbw_ref, z_ref,
                      ubuf, wbuf, cbuf, *, tt, width, n_tiles, ka, kb, row_chunk, ln_chunk):
    i = pl.program_id(1)
    has_prev = i > 0
    has_next = i < n_tiles - 1
    n_lane_tiles = width // LANES
    for c in range(n_lane_tiles):
        cs = slice(c * LANES, (c + 1) * LANES)
        ubuf[c, 0:CONV_A_HALO, :] = jnp.where(has_prev, up_ref[:, cs], 0.0)
        ubuf[c, CONV_A_HALO:CONV_A_HALO + tt, :] = u_ref[:, cs]
        ubuf[c, CONV_A_HALO + tt:, :] = jnp.where(has_next, un_ref[:, cs], 0.0)
        wbuf[c, 0:CONV_B_HALO, :] = jnp.where(has_prev, wp_ref[:, cs], 0.0)
        wbuf[c, CONV_B_HALO:CONV_B_HALO + tt, :] = w_ref[:, cs]
        wbuf[c, CONV_B_HALO + tt:, :] = jnp.where(has_next, wn_ref[:, cs], 0.0)

    off_a = CONV_A_HALO - ka // 2
    off_b = CONV_B_HALO - kb // 2

    for c in range(n_lane_tiles):
        cs = slice(c * LANES, (c + 1) * LANES)

        def conv_rows(r, carry, c=c, cs=cs):
            r0 = pl.multiple_of(r * row_chunk, row_chunk)
            bias = jnp.broadcast_to(cab_ref[:, cs], (F32_SUBLANES, LANES))
            accs = [bias] * (row_chunk // F32_SUBLANES)
            for k in range(ka):
                wk = caw_ref[k:k + 1, cs]
                accs = [a + ubuf[c, pl.ds(r0 + (off_a + k + F32_SUBLANES * s), F32_SUBLANES), :] * wk
                        for s, a in enumerate(accs)]
            for s, a in enumerate(accs):
                cbuf[pl.ds(r0 + F32_SUBLANES * s, F32_SUBLANES), cs] = a
            return carry

        lax.fori_loop(0, tt // row_chunk, conv_rows, 0)

    def norm_rows(r, carry):
        r0 = pl.multiple_of(r * ln_chunk, ln_chunk)
        rows = pl.ds(r0, ln_chunk)
        xc = cbuf[rows, :]
        mu = jnp.mean(xc, axis=-1, keepdims=True)
        dev = xc - mu
        var = jnp.mean(dev * dev, axis=-1, keepdims=True)
        y = dev * lax.rsqrt(var + LN_EPS) * ng_ref[...] + nbias_ref[...]
        z_ref[rows, 0:width] = (_silu(y) * ga_ref[rows, :]).astype(BF16)
        for c in range(n_lane_tiles):
            cs = slice(c * LANES, (c + 1) * LANES)
            parts = []
            for s in range(ln_chunk // F32_SUBLANES):
                v = None
                for k in range(kb):
                    term = (wbuf[c, pl.ds(r0 + (off_b + k + F32_SUBLANES * s), F32_SUBLANES), :]
                            * cbw_ref[k:k + 1, cs])
                    v = term if v is None else v + term
                parts.append(v)
            v = jnp.concatenate(parts, axis=0)
            z_ref[rows, width + c * LANES:width + (c + 1) * LANES] = (gb_ref[rows, cs] * v).astype(BF16)
        return carry

    lax.fori_loop(0, tt // ln_chunk, norm_rows, 0, unroll=2)


def _even_conv(u, ga, w, gb, caw, cab, ng, nbias, cbw):
    nb, t, width = u.shape
    ka, kb = caw.shape[0], cbw.shape[0]
    assert ka // 2 <= CONV_A_HALO and kb // 2 <= CONV_B_HALO
    tt = _pick(t, (256, 128))
    n_tiles = t // tt
    ha, hb = tt // CONV_A_HALO, tt // CONV_B_HALO

    def cur(b, i):
        return (b, i, 0)

    def halo_specs(halo, per_tile):
        last = t // halo - 1
        return [pl.BlockSpec((None, halo, width), lambda b, i: (b, jnp.maximum(i * per_tile - 1, 0), 0)),
                pl.BlockSpec((None, halo, width), lambda b, i: (b, jnp.minimum((i + 1) * per_tile, last), 0))]

    tile = pl.BlockSpec((None, tt, width), cur)
    vec = lambda rows: pl.BlockSpec((rows, width), lambda b, i: (0, 0))
    kern = functools.partial(_even_conv_kernel, tt=tt, width=width, n_tiles=n_tiles, ka=ka, kb=kb,
                             row_chunk=32, ln_chunk=BF16_SUBLANES)
    return pl.pallas_call(
        kern,
        grid=(nb, n_tiles),
        in_specs=[tile] + halo_specs(CONV_A_HALO, ha) + [tile] + halo_specs(CONV_B_HALO, hb) + [tile, tile]
                 + [vec(ka), vec(1), vec(1), vec(1), vec(kb)],
        out_specs=pl.BlockSpec((None, tt, 2 * width), cur),
        out_shape=jax.ShapeDtypeStruct((nb, t, 2 * width), BF16),
        scratch_shapes=[pltpu.VMEM((width // LANES, tt + 2 * CONV_A_HALO, LANES), F32),
                        pltpu.VMEM((width // LANES, tt + 2 * CONV_B_HALO, LANES), F32),
                        pltpu.VMEM((tt, width), F32)],
        compiler_params=_cparams("parallel", "parallel"),
        name="even_conv",
    )(u, u, u, w, w, w, ga, gb, caw, cab.reshape(1, width), ng.reshape(1, width), nbias.reshape(1, width), cbw)


def _outproj_ln_kernel(z_ref, w_ref, x_ref, gate_ref, g_ref, b_ref, o_ref, *, alpha):
    y = jnp.dot(z_ref[...], w_ref[...], preferred_element_type=F32)
    r = alpha * x_ref[...] + gate_ref[...] * y
    mu = jnp.mean(r, axis=-1, keepdims=True)
    dev = r - mu
    var = jnp.mean(dev * dev, axis=-1, keepdims=True)
    o_ref[...] = dev * lax.rsqrt(var + LN_EPS) * g_ref[...] + b_ref[...]


def _outproj_ln(z, w_out, li, x, mod5, layer, row_of, ln_g, ln_b, alpha):
    nb, t, d = x.shape
    kz = z.shape[-1]
    tm = _pick(t, (512, 256, 128))
    tile = lambda b, i: (b, i, 0)
    const = lambda b, i: (0, 0)
    return pl.pallas_call(
        functools.partial(_outproj_ln_kernel, alpha=alpha),
        grid=(nb, t // tm),
        in_specs=[pl.BlockSpec((None, tm, kz), tile),
                  pl.BlockSpec((None, kz, d), lambda b, i: (li, 0, 0), pipeline_mode=pl.Buffered(1)),
                  pl.BlockSpec((None, tm, d), tile),
                  _mod_spec(layer, 2, row_of, d),
                  pl.BlockSpec((1, d), const), pl.BlockSpec((1, d), const)],
        out_specs=pl.BlockSpec((None, tm, d), tile),
        out_shape=jax.ShapeDtypeStruct((nb, t, d), F32),
        compiler_params=_cparams("parallel", "parallel"),
        name="outproj_ln",
    )(z, w_out, x, mod5, ln_g.reshape(1, d), ln_b.reshape(1, d))


def _odd_inproj_kernel(*refs, rope, need_q, q_scale):
    x_ref, sh_ref, sc_ref = refs[:3]
    refs = refs[3:]
    if need_q:
        wq_ref, wg_ref = refs[:2]
        refs = refs[2:]
    wk_ref, wv_ref = refs[:2]
    refs = refs[2:]
    if need_q:
        qg_ref = refs[0]
        refs = refs[1:]
    kg_ref = refs[0]
    refs = refs[1:]
    if rope:
        cos_ref, sa_ref, sb_ref = refs[:3]
        refs = refs[3:]
    if need_q:
        q_ref, sg_ref = refs[:2]
        refs = refs[2:]
    k_ref, vt_ref, h_scr = refs

    @pl.when(pl.program_id(2) == 0)
    def _():
        h_scr[...] = (x_ref[...] * (1.0 + sc_ref[...]) + sh_ref[...]).astype(BF16)

    h = h_scr[...]

    def proj(w):
        return jnp.dot(h, w[...], preferred_element_type=F32)

    def norm_rope(t, g):
        ms = jnp.mean(t * t, axis=-1, keepdims=True)
        t = t * lax.rsqrt(ms + RMS_EPS) * g
        if rope:
            t = (t * cos_ref[...] + pltpu.roll(t, HEAD_DIM - HEAD_DIM // 4, 1) * sa_ref[...]
                 + pltpu.roll(t, HEAD_DIM // 4, 1) * sb_ref[...])
        return t

    k_ref[...] = norm_rope(proj(wk_ref), kg_ref[...]).astype(BF16)
    v = proj(wv_ref)
    for s in range(v.shape[0] // LANES):
        vt_ref[s] = v[s * LANES:(s + 1) * LANES, :].T.astype(BF16)
    if need_q:
        q = proj(wq_ref)
        for hh in range(GQA_GROUP):
            hs = slice(hh * HEAD_DIM, (hh + 1) * HEAD_DIM)
            q_ref[:, hs] = (norm_rope(q[:, hs], qg_ref[...]) * q_scale).astype(BF16)
        sg_ref[...] = _silu(proj(wg_ref))


def _odd_inproj(x, mod5, layer, row_of, w_in, li, q_gain, k_gain, rope_tabs, need_q, q_scale):
    nb, t, d = x.shape
    kvh = w_in.shape[2] // (2 * GQA_GROUP * HEAD_DIM + 2 * HEAD_DIM)
    dq = kvh * GQA_GROUP * HEAD_DIM
    dkv = kvh * HEAD_DIM
    gw = GQA_GROUP * HEAD_DIM
    tm = _pick(t, (512, 256, 128))
    rope = rope_tabs is not None
    idx = lambda b, i, j: (b, i, 0)
    in_specs = [pl.BlockSpec((None, tm, d), idx), _mod_spec(layer, 0, row_of, d), _mod_spec(layer, 1, row_of, d)]
    args = [x, mod5, mod5]
    if need_q:
        in_specs += [pl.BlockSpec((None, d, gw), lambda b, i, j: (li, 0, j)),
                     pl.BlockSpec((None, d, gw), lambda b, i, j: (li, 0, kvh + j))]
        args += [w_in, w_in]
    koff, voff = 2 * dq // HEAD_DIM, 2 * dq // HEAD_DIM + kvh
    in_specs += [pl.BlockSpec((None, d, HEAD_DIM), lambda b, i, j: (li, 0, koff + j)),
                 pl.BlockSpec((None, d, HEAD_DIM), lambda b, i, j: (li, 0, voff + j))]
    args += [w_in, w_in]
    gain_spec = pl.BlockSpec((1, HEAD_DIM), lambda b, i, j: (0, 0))
    if need_q:
        in_specs.append(gain_spec)
        args.append(q_gain.reshape(1, HEAD_DIM))
    in_specs.append(gain_spec)
    args.append(k_gain.reshape(1, HEAD_DIM))
    if rope:
        in_specs += [pl.BlockSpec((tm, HEAD_DIM), lambda b, i, j: (i, 0))] * 3
        args += list(rope_tabs)
    out_specs, out_shape = [], []
    if need_q:
        out_specs += [pl.BlockSpec((None, tm, gw), lambda b, i, j: (b, i, j))] * 2
        out_shape += [jax.ShapeDtypeStruct((nb, t, dq), BF16), jax.ShapeDtypeStruct((nb, t, dq), F32)]
    out_specs += [pl.BlockSpec((None, tm, HEAD_DIM), lambda b, i, j: (b, i, j)),
                  pl.BlockSpec((None, None, tm // LANES, HEAD_DIM, LANES), lambda b, i, j: (b, j, i, 0, 0))]
    out_shape += [jax.ShapeDtypeStruct((nb, t, dkv), BF16),
                  jax.ShapeDtypeStruct((nb, kvh, t // LANES, HEAD_DIM, LANES), BF16)]
    return pl.pallas_call(
        functools.partial(_odd_inproj_kernel, rope=rope, need_q=need_q, q_scale=q_scale),
        grid=(nb, t // tm, kvh),
        in_specs=in_specs,
        out_specs=out_specs,
        out_shape=out_shape,
        scratch_shapes=[pltpu.VMEM((tm, d), BF16)],
        compiler_params=_cparams("parallel", "parallel", "arbitrary"),
        name="odd_inproj",
    )(*args)


def _flash_kernel(q_ref, k_ref, vt_ref, sg_ref, o_ref, acc, m_scr, l_scr, s0, s1, p0, p1, a0, a1, c0, c1,
                  *, tq, tk, n_chunks):
    slabs = tk // LANES
    s_buf, p_buf, a_buf, cmax_buf = (s0, s1), (p0, p1), (a0, a1), (c0, c1)
    m_scr[...] = jnp.full(m_scr.shape, -jnp.inf, F32)
    l_scr[...] = jnp.zeros(l_scr.shape, F32)
    acc[...] = jnp.zeros(acc.shape, F32)

    def qk(c, par, hh):
        qh = q_ref[:, hh * HEAD_DIM:(hh + 1) * HEAD_DIM]
        kc = k_ref[pl.ds(pl.multiple_of(c * tk, tk), tk), :]
        st = lax.dot_general(kc, qh, (((1,), (1,)), ((), ())), preferred_element_type=F32)
        s_buf[par][hh] = st
        cmax_buf[par][hh] = jnp.max(st, axis=0, keepdims=True)

    def softmax(par, hh):
        m_old = m_scr[hh]
        m_new = jnp.maximum(m_old, cmax_buf[par][hh])
        p_buf[par][hh] = jnp.exp2(s_buf[par][hh] - m_new).astype(BF16)
        a_buf[par][hh] = jnp.exp2(m_old - m_new)
        m_scr[hh] = m_new

    def pv(c, par, hh):
        vt = jnp.concatenate([vt_ref[c * slabs + s] for s in range(slabs)], axis=1)
        vt = jnp.concatenate([vt, jnp.ones((BF16_SUBLANES, tk), BF16)], axis=0)
        r = jnp.dot(vt, p_buf[par][hh], preferred_element_type=F32)
        alpha = a_buf[par][hh]
        acc[hh] = alpha * acc[hh] + r[:HEAD_DIM]
        l_scr[hh] = alpha * l_scr[hh] + r[HEAD_DIM:HEAD_DIM + 1]

    def time_step(t, par, do_qk=True, do_softmax=True):
        for hh in range(GQA_GROUP):
            if do_qk:
                qk(t + 2, par, hh)
            if do_softmax:
                softmax(1 - par, hh)
            pv(t, par, hh)

    def two_steps(i, carry):
        time_step(2 * i, 0)
        time_step(2 * i + 1, 1)
        return carry

    for hh in range(GQA_GROUP):
        qk(0, 0, hh)
    if n_chunks > 1:
        for hh in range(GQA_GROUP):
            qk(1, 1, hh)
    for hh in range(GQA_GROUP):
        softmax(0, hh)
    n_steady = max(n_chunks - 2, 0)
    lax.fori_loop(0, n_steady // 2, two_steps, 0)
    if n_steady % 2:
        time_step(n_steady - 1, 0)
    if n_chunks > 1:
        time_step(n_chunks - 2, (n_chunks - 2) % 2, do_qk=False)
    time_step(n_chunks - 1, (n_chunks - 1) % 2, do_qk=False, do_softmax=False)
    for hh in range(GQA_GROUP):
        hs = slice(hh * HEAD_DIM, (hh + 1) * HEAD_DIM)
        o = (acc[hh] * (1.0 / l_scr[hh])).T
        o_ref[:, hs] = (o * sg_ref[:, hs]).astype(BF16)


def _flash(q, k, vt, sg):
    nb, t, dq = q.shape
    n = k.shape[1]
    kvh = vt.shape[1]
    gw = GQA_GROUP * HEAD_DIM
    tq = _pick(t, (256, 128))
    tk = _pick(n, (768, 512, 256, 128))
    qspec = pl.BlockSpec((None, tq, gw), lambda b, g, i: (b, i, g))
    return pl.pallas_call(
        functools.partial(_flash_kernel, tq=tq, tk=tk, n_chunks=n // tk),
        grid=(nb, kvh, t // tq),
        in_specs=[qspec,
                  pl.BlockSpec((None, n, HEAD_DIM), lambda b, g, i: (b, 0, g)),
                  pl.BlockSpec((None, None, n // LANES, HEAD_DIM, LANES), lambda b, g, i: (b, g, 0, 0, 0)),
                  qspec],
        out_specs=qspec,
        out_shape=jax.ShapeDtypeStruct((nb, t, dq), BF16),
        scratch_shapes=[pltpu.VMEM((GQA_GROUP, HEAD_DIM, tq), F32),
                        pltpu.VMEM((GQA_GROUP, 1, tq), F32),
                        pltpu.VMEM((GQA_GROUP, 1, tq), F32),
                        pltpu.VMEM((GQA_GROUP, tk, tq), F32),
                        pltpu.VMEM((GQA_GROUP, tk, tq), F32),
                        pltpu.VMEM((GQA_GROUP, tk, tq), BF16),
                        pltpu.VMEM((GQA_GROUP, tk, tq), BF16),
                        pltpu.VMEM((GQA_GROUP, 1, tq), F32),
                        pltpu.VMEM((GQA_GROUP, 1, tq), F32),
                        pltpu.VMEM((GQA_GROUP, 1, tq), F32),
                        pltpu.VMEM((GQA_GROUP, 1, tq), F32)],
        compiler_params=_cparams("parallel", "parallel", "parallel"),
        name="flash_gqa",
    )(q, k, vt, sg)


def _rope_tables(n_tokens):
    quarter = HEAD_DIM // 4
    pos = jnp.arange(n_tokens, dtype=jnp.int32)
    row = (pos // GRID_W).astype(F32)
    col = (pos % GRID_W).astype(F32)
    inv_freq = ROPE_THETA ** (-jnp.arange(0, HEAD_DIM // 2, 2, dtype=F32) / (HEAD_DIM // 2))
    ang_r = row[:, None] * inv_freq[None, :]
    ang_c = col[:, None] * inv_freq[None, :]
    cr, sr, cc, sc = jnp.cos(ang_r), jnp.sin(ang_r), jnp.cos(ang_c), jnp.sin(ang_c)
    zero = jnp.zeros((n_tokens, quarter), F32)
    cos = jnp.concatenate([cr, cr, cc, cc], axis=-1)
    sin_a = jnp.concatenate([-sr, zero, -sc, zero], axis=-1)
    sin_b = jnp.concatenate([zero, sr, zero, sc], axis=-1)
    return cos, sin_a, sin_b


@jax.jit
def kernel(x, c, ctx, c_ctx, w_mod, b_mod, post_ln_g, post_ln_b, w_in_e, conv_a_w, conv_a_b, norm_a_g,
           norm_a_b, conv_b_w, w_out_e, w_in_o, q_norm_g, k_norm_g, w_out_o):
    nb, seq, _ = x.shape
    depth = w_mod.shape[0]
    alpha = (2.0 * depth) ** 0.25
    q_scale = HEAD_DIM ** -0.5 * math.log2(math.e)
    lat_row = lambda b: b
    ctx_row = lambda b: nb

    mod5 = _adaln_all(c, c_ctx, w_mod, b_mod)
    w_in_e, w_out_e, w_in_o, w_out_o = (w.astype(BF16) for w in (w_in_e, w_out_e, w_in_o, w_out_o))
    rope_tabs = _rope_tables(seq)

    for layer in range(depth):
        last = layer == depth - 1
        i = layer // 2
        ln_g, ln_b = post_ln_g[layer], post_ln_b[layer]
        if layer % 2 == 0:
            def conv_mix(s, row_of):
                u, ga, w, gb = _even_inproj(s, mod5, layer, row_of, w_in_e, i)
                z = _even_conv(u, ga, w, gb, conv_a_w[i], conv_a_b[i], norm_a_g[i], norm_a_b[i], conv_b_w[i])
                return _outproj_ln(z, w_out_e, i, s, mod5, layer, row_of, ln_g, ln_b, alpha)

            x_new = conv_mix(x, lat_row)
            if not last:
                ctx = conv_mix(ctx, ctx_row)
            x = x_new
        else:
            q_l, sg_l, k_l, vt_l = _odd_inproj(x, mod5, layer, lat_row, w_in_o, i, q_norm_g[i], k_norm_g[i],
                                               rope_tabs, True, q_scale)
            ctx_out = _odd_inproj(ctx, mod5, layer, ctx_row, w_in_o, i, q_norm_g[i], k_norm_g[i],
                                  None, not last, q_scale)
            k_c, vt_c = ctx_out[-2:]
            k_all = jnp.concatenate([k_c, k_l], axis=1)
            vt_all = jnp.concatenate([vt_c, vt_l], axis=2)
            z_l = _flash(q_l, k_all, vt_all, sg_l)
            x = _outproj_ln(z_l, w_out_o, i, x, mod5, layer, lat_row, ln_g, ln_b, alpha)
            if not last:
                q_c, sg_c = ctx_out[:2]
                z_c = _flash(q_c, k_c, vt_c, sg_c)
                ctx = _outproj_ln(z_c, w_out_o, i, ctx, mod5, layer, ctx_row, ln_g, ln_b, alpha)
    return x
```

```python
import functools
import math

import jax
import jax.numpy as jnp
from jax import lax
from jax.experimental import pallas as pl
from jax.experimental.pallas import tpu as pltpu

F32 = jnp.float32
BF16 = jnp.bfloat16

HEAD_DIM = 128
GQA_GROUP = 4
GRID_W = 64
ROPE_THETA = 10000.0
LN_EPS = 1e-5
RMS_EPS = 1e-6

LANES = 128
F32_SUBLANES = 8
BF16_SUBLANES = 16
VMEM_LIMIT_BYTES = 56 * 1024 * 1024

MOD_ROWS = 8
CONV_A_HALO = 16
CONV_B_HALO = 8


def _pick(n, cands):
    for c in cands:
        if n % c == 0:
            return c
    return n


def _sigmoid(x):
    return 1.0 / (1.0 + jnp.exp(-x))


def _silu(x):
    return x * _sigmoid(x)


def _cparams(*sem):
    return pltpu.CompilerParams(dimension_semantics=sem, vmem_limit_bytes=VMEM_LIMIT_BYTES)


def _adaln_kernel(c_ref, w_ref, b_ref, o_ref):
    s = _silu(c_ref[...]).astype(BF16)
    w = w_ref[...].astype(BF16)
    o_ref[...] = jnp.dot(s, w, preferred_element_type=F32) + b_ref[...]


def _adaln_all(c, c_ctx, w_mod, b_mod):
    n_layers, d, d3 = w_mod.shape
    nb = c.shape[0]
    cc = jnp.zeros((MOD_ROWS, d), F32).at[:nb].set(c).at[nb].set(c_ctx)
    tn = _pick(d3, (1024, 512, 256, 128))
    out = pl.pallas_call(
        _adaln_kernel,
        grid=(n_layers, d3 // tn),
        in_specs=[
            pl.BlockSpec((MOD_ROWS, d), lambda l, j: (0, 0)),
            pl.BlockSpec((None, d, tn), lambda l, j: (l, 0, j)),
            pl.BlockSpec((None, 1, tn), lambda l, j: (l, 0, j)),
        ],
        out_specs=pl.BlockSpec((None, MOD_ROWS, tn), lambda l, j: (l, 0, j)),
        out_shape=jax.ShapeDtypeStruct((n_layers, MOD_ROWS, d3), F32),
        compiler_params=_cparams("parallel", "parallel"),
        name="adaln_mod",
    )(cc, w_mod, b_mod.reshape(n_layers, 1, d3))
    return out.reshape(n_layers, MOD_ROWS, 3, d).transpose(0, 2, 1, 3).reshape(n_layers, 3, MOD_ROWS, 1, d)


def _mod_spec(layer, chunk, row_of, d):
    return pl.BlockSpec((None, None, None, 1, d), lambda b, *_: (layer, chunk, row_of(b), 0, 0))


def _even_inproj_kernel(x_ref, sh_ref, sc_ref, w_aval, w_aglu, w_agate, w_bx, w_bb, w_bc, w_bgate,
                        u_ref, ga_ref, w_ref, gb_ref, h_scr):
    @pl.when(pl.program_id(2) == 0)
    def _():
        h_scr[...] = (x_ref[...] * (1.0 + sc_ref[...]) + sh_ref[...]).astype(BF16)

    h = h_scr[...]

    def proj(w):
        return jnp.dot(h, w[...], preferred_element_type=F32)

    u_ref[...] = proj(w_aval) * _sigmoid(proj(w_aglu))
    ga_ref[...] = _silu(proj(w_agate))
    w_ref[...] = proj(w_bc) * proj(w_bx)
    gb_ref[...] = proj(w_bb) * _silu(proj(w_bgate))


def _even_inproj(x, mod5, layer, row_of, w_in, li):
    nb, t, d = x.shape
    width = w_in.shape[2] // 7
    tm = _pick(t, (512, 256, 128))
    tn = _pick(width, (256, 128))
    ncol = width // tn
    w_specs = [pl.BlockSpec((None, d, tn), lambda b, i, j, c=c: (li, 0, c * ncol + j)) for c in range(7)]
    out_spec = pl.BlockSpec((None, tm, tn), lambda b, i, j: (b, i, j))
    out_sds = jax.ShapeDtypeStruct((nb, t, width), F32)
    return pl.pallas_call(
        _even_inproj_kernel,
        grid=(nb, t // tm, ncol),
        in_specs=[pl.BlockSpec((None, tm, d), lambda b, i, j: (b, i, 0)),
                  _mod_spec(layer, 0, row_of, d), _mod_spec(layer, 1, row_of, d)] + w_specs,
        out_specs=[out_spec] * 4,
        out_shape=[out_sds] * 4,
        scratch_shapes=[pltpu.VMEM((tm, d), BF16)],
        compiler_params=_cparams("parallel", "parallel", "arbitrary"),
        name="even_inproj",
    )(x, mod5, mod5, *([w_in] * 7))


def _even_conv_kernel(u_ref, up_ref, un_ref, w_ref, wp_ref, wn_ref, ga_ref, gb_ref,
                      caw_ref, cab_ref, ng_ref, nbias_ref, cbw_ref, z_ref,
                      ubuf, wbuf, cbuf, *, tt, width, n_tiles, ka, kb, row_chunk, ln_chunk):
    i = pl.program_id(1)
    has_prev = i > 0
    has_next = i < n_tiles - 1
    n_lane_tiles = width // LANES
    for c in range(n_lane_tiles):
        cs = slice(c * LANES, (c + 1) * LANES)
        ubuf[c, 0:CONV_A_HALO, :] = jnp.where(has_prev, up_ref[:, cs], 0.0)
        ubuf[c, CONV_A_HALO:CONV_A_HALO + tt, :] = u_ref[:, cs]
        ubuf[c, CONV_A_HALO + tt:, :] = jnp.where(has_next, un_ref[:, cs], 0.0)
        wbuf[c, 0:CONV_B_HALO, :] = jnp.where(has_prev, wp_ref[:, cs], 0.0)
        wbuf[c, CONV_B_HALO:CONV_B_HALO + tt, :] = w_ref[:, cs]
        wbuf[c, CONV_B_HALO + tt:, :] = jnp.where(has_next, wn_ref[:, cs], 0.0)

    off_a = CONV_A_HALO - ka // 2
    off_b = CONV_B_HALO - kb // 2

    for c in range(n_lane_tiles):
        cs = slice(c * LANES, (c + 1) * LANES)

        def conv_rows(r, carry, c=c, cs=cs):
            r0 = pl.multiple_of(r * row_chunk, row_chunk)
            bias = jnp.broadcast_to(cab_ref[:, cs], (F32_SUBLANES, LANES))
            accs = [bias] * (row_chunk // F32_SUBLANES)
            for k in range(ka):
                wk = caw_ref[k:k + 1, cs]
                accs = [a + ubuf[c, pl.ds(r0 + (off_a + k + F32_SUBLANES * s), F32_SUBLANES), :] * wk
                        for s, a in enumerate(accs)]
            for s, a in enumerate(accs):
                cbuf[pl.ds(r0 + F32_SUBLANES * s, F32_SUBLANES), cs] = a
            return carry

        lax.fori_loop(0, tt // row_chunk, conv_rows, 0)

    def norm_rows(r, carry):
        r0 = pl.multiple_of(r * ln_chunk, ln_chunk)
        rows = pl.ds(r0, ln_chunk)
        xc = cbuf[rows, :]
        mu = jnp.mean(xc, axis=-1, keepdims=True)
        dev = xc - mu
        var = jnp.mean(dev * dev, axis=-1, keepdims=True)
        y = dev * lax.rsqrt(var + LN_EPS) * ng_ref[...] + nbias_ref[...]
        z_ref[rows, 0:width] = (_silu(y) * ga_ref[rows, :]).astype(BF16)
        for c in range(n_lane_tiles):
            cs = slice(c * LANES, (c + 1) * LANES)
            parts = []
            for s in range(ln_chunk // F32_SUBLANES):
                v = None
                for k in range(kb):
                    term = (wbuf[c, pl.ds(r0 + (off_b + k + F32_SUBLANES * s), F32_SUBLANES), :]
                            * cbw_ref[k:k + 1, cs])
                    v = term if v is None else v + term
                parts.append(v)
            v = jnp.concatenate(parts, axis=0)
            z_ref[rows, width + c * LANES:width + (c + 1) * LANES] = (gb_ref[rows, cs] * v).astype(BF16)
        return carry

    lax.fori_loop(0, tt // ln_chunk, norm_rows, 0, unroll=4)


def _even_conv(u, ga, w, gb, caw, cab, ng, nbias, cbw):
    nb, t, width = u.shape
    ka, kb = caw.shape[0], cbw.shape[0]
    assert ka // 2 <= CONV_A_HALO and kb // 2 <= CONV_B_HALO
    tt = _pick(t, (256, 128))
    n_tiles = t // tt
    ha, hb = tt // CONV_A_HALO, tt // CONV_B_HALO

    def cur(b, i):
        return (b, i, 0)

    def halo_specs(halo, per_tile):
        last = t // halo - 1
        return [pl.BlockSpec((None, halo, width), lambda b, i: (b, jnp.maximum(i * per_tile - 1, 0), 0)),
                pl.BlockSpec((None, halo, width), lambda b, i: (b, jnp.minimum((i + 1) * per_tile, last), 0))]

    tile = pl.BlockSpec((None, tt, width), cur)
    vec = lambda rows: pl.BlockSpec((rows, width), lambda b, i: (0, 0))
    kern = functools.partial(_even_conv_kernel, tt=tt, width=width, n_tiles=n_tiles, ka=ka, kb=kb,
                             row_chunk=32, ln_chunk=BF16_SUBLANES)
    return pl.pallas_call(
        kern,
        grid=(nb, n_tiles),
        in_specs=[tile] + halo_specs(CONV_A_HALO, ha) + [tile] + halo_specs(CONV_B_HALO, hb) + [tile, tile]
                 + [vec(ka), vec(1), vec(1), vec(1), vec(kb)],
        out_specs=pl.BlockSpec((None, tt, 2 * width), cur),
        out_shape=jax.ShapeDtypeStruct((nb, t, 2 * width), BF16),
        scratch_shapes=[pltpu.VMEM((width // LANES, tt + 2 * CONV_A_HALO, LANES), F32),
                        pltpu.VMEM((width // LANES, tt + 2 * CONV_B_HALO, LANES), F32),
                        pltpu.VMEM((tt, width), F32)],
        compiler_params=_cparams("parallel", "parallel"),
        name="even_conv",
    )(u, u, u, w, w, w, ga, gb, caw, cab.reshape(1, width), ng.reshape(1, width), nbias.reshape(1, width), cbw)


def _outproj_ln_kernel(z_ref, w_ref, x_ref, gate_ref, g_ref, b_ref, o_ref, *, alpha):
    tm = x_ref.shape[0]
    rb = _pick(tm, (256, 128))
    for blk in range(tm // rb):
        rows = slice(blk * rb, (blk + 1) * rb)
        y = jnp.dot(z_ref[rows, :], w_ref[...], preferred_element_type=F32)
        r = alpha * x_ref[rows, :] + gate_ref[...] * y
        mu = jnp.mean(r, axis=-1, keepdims=True)
        dev = r - mu
        var = jnp.mean(dev * dev, axis=-1, keepdims=True)
        o_ref[rows, :] = dev * lax.rsqrt(var + LN_EPS) * g_ref[...] + b_ref[...]


def _outproj_ln(z, w_out, li, x, mod5, layer, row_of, ln_g, ln_b, alpha):
    nb, t, d = x.shape
    kz = z.shape[-1]
    tm = _pick(t, (512, 256, 128))
    tile = lambda b, i: (b, i, 0)
    const = lambda b, i: (0, 0)
    return pl.pallas_call(
        functools.partial(_outproj_ln_kernel, alpha=alpha),
        grid=(nb, t // tm),
        in_specs=[pl.BlockSpec((None, tm, kz), tile),
                  pl.BlockSpec((None, kz, d), lambda b, i: (li, 0, 0), pipeline_mode=pl.Buffered(1)),
                  pl.BlockSpec((None, tm, d), tile),
                  _mod_spec(layer, 2, row_of, d),
                  pl.BlockSpec((1, d), const), pl.BlockSpec((1, d), const)],
        out_specs=pl.BlockSpec((None, tm, d), tile),
        out_shape=jax.ShapeDtypeStruct((nb, t, d), F32),
        compiler_params=_cparams("parallel", "parallel"),
        name="outproj_ln",
    )(z, w_out, x, mod5, ln_g.reshape(1, d), ln_b.reshape(1, d))


def _odd_inproj_kernel(*refs, rope, need_q, q_scale):
    x_ref, sh_ref, sc_ref = refs[:3]
    refs = refs[3:]
    if need_q:
        wq_ref, wg_ref = refs[:2]
        refs = refs[2:]
    wk_ref, wv_ref = refs[:2]
    refs = refs[2:]
    if need_q:
        qg_ref = refs[0]
        refs = refs[1:]
    kg_ref = refs[0]
    refs = refs[1:]
    if rope:
        cos_ref, sa_ref, sb_ref = refs[:3]
        refs = refs[3:]
    if need_q:
        q_ref, sg_ref = refs[:2]
        refs = refs[2:]
    k_ref, vt_ref, h_scr = refs

    @pl.when(pl.program_id(2) == 0)
    def _():
        h_scr[...] = (x_ref[...] * (1.0 + sc_ref[...]) + sh_ref[...]).astype(BF16)

    tm = h_scr.shape[0]
    w_kv = jnp.concatenate([wk_ref[...], wv_ref[...]], axis=1)

    def norm_rope(t, g, rows):
        ms = jnp.mean(t * t, axis=-1, keepdims=True)
        t = t * lax.rsqrt(ms + RMS_EPS) * g
        if rope:
            t = (t * cos_ref[rows, :] + pltpu.roll(t, HEAD_DIM - HEAD_DIM // 4, 1) * sa_ref[rows, :]
                 + pltpu.roll(t, HEAD_DIM // 4, 1) * sb_ref[rows, :])
        return t

    rb = _pick(tm, (256, 128))
    for r in range(tm // rb):
        rows = slice(r * rb, (r + 1) * rb)
        h = h_scr[rows, :]
        kv = jnp.dot(h, w_kv, preferred_element_type=F32)
        k_ref[rows, :] = norm_rope(kv[:, :HEAD_DIM], kg_ref[...], rows).astype(BF16)
        v = kv[:, HEAD_DIM:]
        for s in range(rb // LANES):
            vt_ref[r * (rb // LANES) + s] = v[s * LANES:(s + 1) * LANES, :].T.astype(BF16)
        if need_q:
            q = jnp.dot(h, wq_ref[...], preferred_element_type=F32)
            for hh in range(GQA_GROUP):
                hs = slice(hh * HEAD_DIM, (hh + 1) * HEAD_DIM)
                q_ref[rows, hs] = (norm_rope(q[:, hs], qg_ref[...], rows) * q_scale).astype(BF16)
            sg_ref[rows, :] = _silu(jnp.dot(h, wg_ref[...], preferred_element_type=F32))


def _odd_inproj(x, mod5, layer, row_of, w_in, li, q_gain, k_gain, rope_tabs, need_q, q_scale):
    nb, t, d = x.shape
    kvh = w_in.shape[2] // (2 * GQA_GROUP * HEAD_DIM + 2 * HEAD_DIM)
    dq = kvh * GQA_GROUP * HEAD_DIM
    dkv = kvh * HEAD_DIM
    gw = GQA_GROUP * HEAD_DIM
    tm = _pick(t, (512, 256, 128))
    rope = rope_tabs is not None
    idx = lambda b, i, j: (b, i, 0)
    in_specs = [pl.BlockSpec((None, tm, d), idx), _mod_spec(layer, 0, row_of, d), _mod_spec(layer, 1, row_of, d)]
    args = [x, mod5, mod5]
    if need_q:
        in_specs += [pl.BlockSpec((None, d, gw), lambda b, i, j: (li, 0, j)),
                     pl.BlockSpec((None, d, gw), lambda b, i, j: (li, 0, kvh + j))]
        args += [w_in, w_in]
    koff, voff = 2 * dq // HEAD_DIM, 2 * dq // HEAD_DIM + kvh
    in_specs += [pl.BlockSpec((None, d, HEAD_DIM), lambda b, i, j: (li, 0, koff + j)),
                 pl.BlockSpec((None, d, HEAD_DIM), lambda b, i, j: (li, 0, voff + j))]
    args += [w_in, w_in]
    gain_spec = pl.BlockSpec((1, HEAD_DIM), lambda b, i, j: (0, 0))
    if need_q:
        in_specs.append(gain_spec)
        args.append(q_gain.reshape(1, HEAD_DIM))
    in_specs.append(gain_spec)
    args.append(k_gain.reshape(1, HEAD_DIM))
    if rope:
        in_specs += [pl.BlockSpec((tm, HEAD_DIM), lambda b, i, j: (i, 0))] * 3
        args += list(rope_tabs)
    out_specs, out_shape = [], []
    if need_q:
        out_specs += [pl.BlockSpec((None, tm, gw), lambda b, i, j: (b, i, j))] * 2
        out_shape += [jax.ShapeDtypeStruct((nb, t, dq), BF16), jax.ShapeDtypeStruct((nb, t, dq), F32)]
    out_specs += [pl.BlockSpec((None, tm, HEAD_DIM), lambda b, i, j: (b, i, j)),
                  pl.BlockSpec((None, None, tm // LANES, HEAD_DIM, LANES), lambda b, i, j: (b, j, i, 0, 0))]
    out_shape += [jax.ShapeDtypeStruct((nb, t, dkv), BF16),
                  jax.ShapeDtypeStruct((nb, kvh, t // LANES, HEAD_DIM, LANES), BF16)]
    return pl.pallas_call(
        functools.partial(_odd_inproj_kernel, rope=rope, need_q=need_q, q_scale=q_scale),
        grid=(nb, t // tm, kvh),
        in_specs=in_specs,
        out_specs=out_specs,
        out_shape=out_shape,
        scratch_shapes=[pltpu.VMEM((tm, d), BF16)],
        compiler_params=_cparams("parallel", "parallel", "arbitrary"),
        name="odd_inproj",
    )(*args)


def _flash_kernel(q_ref, k_ref, vt_ref, sg_ref, o_ref, acc, m_scr, l_scr, s0, s1, p0, p1, a0, a1, c0, c1,
                  *, tq, tk, n_chunks):
    slabs = tk // LANES
    s_buf, p_buf, a_buf, cmax_buf = (s0, s1), (p0, p1), (a0, a1), (c0, c1)
    m_scr[...] = jnp.full(m_scr.shape, -jnp.inf, F32)
    l_scr[...] = jnp.zeros(l_scr.shape, F32)
    acc[...] = jnp.zeros(acc.shape, F32)

    def qk(c, par, hh):
        qh = q_ref[:, hh * HEAD_DIM:(hh + 1) * HEAD_DIM]
        kc = k_ref[pl.ds(pl.multiple_of(c * tk, tk), tk), :]
        st = lax.dot_general(kc, qh, (((1,), (1,)), ((), ())), preferred_element_type=F32)
        s_buf[par][hh] = st
        cmax_buf[par][hh] = jnp.max(st, axis=0, keepdims=True)

    def softmax(par, hh):
        m_old = m_scr[hh]
        m_new = jnp.maximum(m_old, cmax_buf[par][hh])
        p_buf[par][hh] = jnp.exp2(s_buf[par][hh] - m_new).astype(BF16)
        a_buf[par][hh] = jnp.exp2(m_old - m_new)
        m_scr[hh] = m_new

    def pv(c, par, hh):
        vt = jnp.concatenate([vt_ref[c * slabs + s] for s in range(slabs)], axis=1)
        vt = jnp.concatenate([vt, jnp.ones((BF16_SUBLANES, tk), BF16)], axis=0)
        r = jnp.dot(vt, p_buf[par][hh], preferred_element_type=F32)
        alpha = a_buf[par][hh]
        acc[hh] = alpha * acc[hh] + r[:HEAD_DIM]
        l_scr[hh] = alpha * l_scr[hh] + r[HEAD_DIM:HEAD_DIM + 1]

    def time_step(t, par, do_qk=True, do_softmax=True):
        for hh in range(GQA_GROUP):
            if do_qk:
                qk(t + 2, par, hh)
            pv(t, par, hh)
        for hh in range(GQA_GROUP):
            if do_softmax:
                softmax(1 - par, hh)

    steps_per_trip = 2

    def steady_steps(i, carry):
        for j in range(steps_per_trip):
            time_step(steps_per_trip * i + j, j % 2)
        return carry

    for hh in range(GQA_GROUP):
        qk(0, 0, hh)
    if n_chunks > 1:
        for hh in range(GQA_GROUP):
            qk(1, 1, hh)
    for hh in range(GQA_GROUP):
        softmax(0, hh)
    n_steady = max(n_chunks - 2, 0)
    n_trips = n_steady // steps_per_trip
    lax.fori_loop(0, n_trips, steady_steps, 0)
    for t in range(n_trips * steps_per_trip, n_steady):
        time_step(t, t % 2)
    if n_chunks > 1:
        time_step(n_chunks - 2, (n_chunks - 2) % 2, do_qk=False)
    time_step(n_chunks - 1, (n_chunks - 1) % 2, do_qk=False, do_softmax=False)
    for hh in range(GQA_GROUP):
        hs = slice(hh * HEAD_DIM, (hh + 1) * HEAD_DIM)
        o = (acc[hh] * (1.0 / l_scr[hh])).T
        o_ref[:, hs] = (o * sg_ref[:, hs]).astype(BF16)


def _flash(q, k, vt, sg):
    nb, t, dq = q.shape
    n = k.shape[1]
    kvh = vt.shape[1]
    gw = GQA_GROUP * HEAD_DIM
    tq = _pick(t, (256, 128))
    tk = _pick(n, (768, 512, 256, 128))
    qspec = pl.BlockSpec((None, tq, gw), lambda b, g, i: (b, i, g))
    return pl.pallas_call(
        functools.partial(_flash_kernel, tq=tq, tk=tk, n_chunks=n // tk),
        grid=(nb, kvh, t // tq),
        in_specs=[qspec,
                  pl.BlockSpec((None, n, HEAD_DIM), lambda b, g, i: (b, 0, g)),
                  pl.BlockSpec((None, None, n // LANES, HEAD_DIM, LANES), lambda b, g, i: (b, g, 0, 0, 0)),
                  qspec],
        out_specs=qspec,
        out_shape=jax.ShapeDtypeStruct((nb, t, dq), BF16),
        scratch_shapes=[pltpu.VMEM((GQA_GROUP, HEAD_DIM, tq), F32),
                        pltpu.VMEM((GQA_GROUP, 1, tq), F32),
                        pltpu.VMEM((GQA_GROUP, 1, tq), F32),
                        pltpu.VMEM((GQA_GROUP, tk, tq), F32),
                        pltpu.VMEM((GQA_GROUP, tk, tq), F32),
                        pltpu.VMEM((GQA_GROUP, tk, tq), BF16),
                        pltpu.VMEM((GQA_GROUP, tk, tq), BF16),
                        pltpu.VMEM((GQA_GROUP, 1, tq), F32),
                        pltpu.VMEM((GQA_GROUP, 1, tq), F32),
                        pltpu.VMEM((GQA_GROUP, 1, tq), F32),
                        pltpu.VMEM((GQA_GROUP, 1, tq), F32)],
        compiler_params=_cparams("parallel", "parallel", "parallel"),
        name="flash_gqa",
    )(q, k, vt, sg)


def _rope_tables(n_tokens):
    quarter = HEAD_DIM // 4
    pos = jnp.arange(n_tokens, dtype=jnp.int32)
    row = (pos // GRID_W).astype(F32)
    col = (pos % GRID_W).astype(F32)
    inv_freq = ROPE_THETA ** (-jnp.arange(0, HEAD_DIM // 2, 2, dtype=F32) / (HEAD_DIM // 2))
    ang_r = row[:, None] * inv_freq[None, :]
    ang_c = col[:, None] * inv_freq[None, :]
    cr, sr, cc, sc = jnp.cos(ang_r), jnp.sin(ang_r), jnp.cos(ang_c), jnp.sin(ang_c)
    zero = jnp.zeros((n_tokens, quarter), F32)
    cos = jnp.concatenate([cr, cr, cc, cc], axis=-1)
    sin_a = jnp.concatenate([-sr, zero, -sc, zero], axis=-1)
    sin_b = jnp.concatenate([zero, sr, zero, sc], axis=-1)
    return cos, sin_a, sin_b


@jax.jit
def kernel(x, c, ctx, c_ctx, w_mod, b_mod, post_ln_g, post_ln_b, w_in_e, conv_a_w, conv_a_b, norm_a_g,
           norm_a_b, conv_b_w, w_out_e, w_in_o, q_norm_g, k_norm_g, w_out_o):
    nb, seq, _ = x.shape
    depth = w_mod.shape[0]
    alpha = (2.0 * depth) ** 0.25
    q_scale = HEAD_DIM ** -0.5 * math.log2(math.e)
    lat_row = lambda b: b
    ctx_row = lambda b: nb

    mod5 = _adaln_all(c, c_ctx, w_mod, b_mod)
    w_in_e, w_out_e, w_in_o, w_out_o = (w.astype(BF16) for w in (w_in_e, w_out_e, w_in_o, w_out_o))
    rope_tabs = _rope_tables(seq)

    for layer in range(depth):
        last = layer == depth - 1
        i = layer // 2
        ln_g, ln_b = post_ln_g[layer], post_ln_b[layer]
        if layer % 2 == 0:
            def conv_mix(s, row_of):
                u, ga, w, gb = _even_inproj(s, mod5, layer, row_of, w_in_e, i)
                z = _even_conv(u, ga, w, gb, conv_a_w[i], conv_a_b[i], norm_a_g[i], norm_a_b[i], conv_b_w[i])
                return _outproj_ln(z, w_out_e, i, s, mod5, layer, row_of, ln_g, ln_b, alpha)

            x_new = conv_mix(x, lat_row)
            if not last:
                ctx = conv_mix(ctx, ctx_row)
            x = x_new
        else:
            q_l, sg_l, k_l, vt_l = _odd_inproj(x, mod5, layer, lat_row, w_in_o, i, q_norm_g[i], k_norm_g[i],
                                               rope_tabs, True, q_scale)
            ctx_out = _odd_inproj(ctx, mod5, layer, ctx_row, w_in_o, i, q_norm_g[i], k_norm_g[i],
                                  None, not last, q_scale)
            k_c, vt_c = ctx_out[-2:]
            k_all = jnp.concatenate([k_c, k_l], axis=1)
            vt_all = jnp.concatenate([vt_c, vt_l], axis=2)
            z_l = _flash(q_l, k_all, vt_all, sg_l)
            x = _outproj_ln(z_l, w_out_o, i, x, mod5, layer, lat_row, ln_g, ln_b, alpha)
            if not last:
                q_c, sg_c = ctx_out[:2]
                z_c = _flash(q_c, k_c, vt_c, sg_c)
                ctx = _outproj_ln(z_c, w_out_o, i, ctx, mod5, layer, ctx_row, ln_g, ln_b, alpha)
    return x
```

```python
import functools
import math

import jax
import jax.numpy as jnp
from jax import lax
from jax.experimental import pallas as pl
from jax.experimental.pallas import tpu as pltpu

F32 = jnp.float32
BF16 = jnp.bfloat16

HEAD_DIM = 128
GQA_GROUP = 4
GRID_W = 64
ROPE_THETA = 10000.0
LN_EPS = 1e-5
RMS_EPS = 1e-6

LANES = 128
F32_SUBLANES = 8
BF16_SUBLANES = 16
VMEM_LIMIT_BYTES = 56 * 1024 * 1024

MOD_ROWS = 8
CONV_A_HALO = 16
CONV_B_HALO = 8


def _pick(n, cands):
    for c in cands:
        if n % c == 0:
            return c
    return n


def _sigmoid(x):
    return 1.0 / (1.0 + jnp.exp(-x))


def _silu(x):
    return x * _sigmoid(x)


def _cparams(*sem):
    return pltpu.CompilerParams(dimension_semantics=sem, vmem_limit_bytes=VMEM_LIMIT_BYTES)


def _adaln_kernel(c_ref, w_ref, b_ref, o_ref):
    s = _silu(c_ref[...]).astype(BF16)
    w = w_ref[...].astype(BF16)
    o_ref[...] = jnp.dot(s, w, preferred_element_type=F32) + b_ref[...]


def _adaln_all(c, c_ctx, w_mod, b_mod):
    n_layers, d, d3 = w_mod.shape
    nb = c.shape[0]
    cc = jnp.zeros((MOD_ROWS, d), F32).at[:nb].set(c).at[nb].set(c_ctx)
    tn = _pick(d3, (1024, 512, 256, 128))
    out = pl.pallas_call(
        _adaln_kernel,
        grid=(n_layers, d3 // tn),
        in_specs=[
            pl.BlockSpec((MOD_ROWS, d), lambda l, j: (0, 0)),
            pl.BlockSpec((None, d, tn), lambda l, j: (l, 0, j)),
            pl.BlockSpec((None, 1, tn), lambda l, j: (l, 0, j)),
        ],
        out_specs=pl.BlockSpec((None, MOD_ROWS, tn), lambda l, j: (l, 0, j)),
        out_shape=jax.ShapeDtypeStruct((n_layers, MOD_ROWS, d3), F32),
        compiler_params=_cparams("parallel", "parallel"),
        name="adaln_mod",
    )(cc, w_mod, b_mod.reshape(n_layers, 1, d3))
    return out.reshape(n_layers, MOD_ROWS, 3, d).transpose(0, 2, 1, 3).reshape(n_layers, 3, MOD_ROWS, 1, d)


def _mod_spec(layer, chunk, row_of, d):
    return pl.BlockSpec((None, None, None, 1, d), lambda b, *_: (layer, chunk, row_of(b), 0, 0))


def _even_inproj_kernel(x_ref, sh_ref, sc_ref, w_aval, w_aglu, w_agate, w_bx, w_bb, w_bc, w_bgate,
                        u_ref, ga_ref, w_ref, gb_ref, h_scr):
    @pl.when(pl.program_id(2) == 0)
    def _():
        h_scr[...] = (x_ref[...] * (1.0 + sc_ref[...]) + sh_ref[...]).astype(BF16)

    h = h_scr[...]

    def proj(w):
        return jnp.dot(h, w[...], preferred_element_type=F32)

    u_ref[...] = proj(w_aval) * _sigmoid(proj(w_aglu))
    ga_ref[...] = _silu(proj(w_agate))
    w_ref[...] = proj(w_bc) * proj(w_bx)
    gb_ref[...] = proj(w_bb) * _silu(proj(w_bgate))


def _even_inproj(x, mod5, layer, row_of, w_in, li):
    nb, t, d = x.shape
    width = w_in.shape[2] // 7
    tm = _pick(t, (1024, 512, 256, 128))
    tn = _pick(width, (256, 128))
    ncol = width // tn
    w_specs = [pl.BlockSpec((None, d, tn), lambda b, i, j, c=c: (li, 0, c * ncol + j)) for c in range(7)]
    out_spec = pl.BlockSpec((None, tm, tn), lambda b, i, j: (b, i, j))
    out_sds = jax.ShapeDtypeStruct((nb, t, width), F32)
    return pl.pallas_call(
        _even_inproj_kernel,
        grid=(nb, t // tm, ncol),
        in_specs=[pl.BlockSpec((None, tm, d), lambda b, i, j: (b, i, 0)),
                  _mod_spec(layer, 0, row_of, d), _mod_spec(layer, 1, row_of, d)] + w_specs,
        out_specs=[out_spec] * 4,
        out_shape=[out_sds] * 4,
        scratch_shapes=[pltpu.VMEM((tm, d), BF16)],
        compiler_params=_cparams("parallel", "parallel", "arbitrary"),
        name="even_inproj",
    )(x, mod5, mod5, *([w_in] * 7))


def _even_conv_kernel(u_ref, up_ref, un_ref, w_ref, wp_ref, wn_ref, ga_ref, gb_ref,
                      caw_ref, cab_ref, ng_ref, nbias_ref, cbw_ref, z_ref,
                      ubuf, wbuf, cbuf, *, tt, width, n_tiles, ka, kb, row_chunk, ln_chunk):
    i = pl.program_id(1)
    has_prev = i > 0
    has_next = i < n_tiles - 1
    n_lane_tiles = width // LANES
    for c in range(n_lane_tiles):
        cs = slice(c * LANES, (c + 1) * LANES)
        ubuf[c, 0:CONV_A_HALO, :] = jnp.where(has_prev, up_ref[:, cs], 0.0)
        ubuf[c, CONV_A_HALO:CONV_A_HALO + tt, :] = u_ref[:, cs]
        ubuf[c, CONV_A_HALO + tt:, :] = jnp.where(has_next, un_ref[:, cs], 0.0)
        wbuf[c, 0:CONV_B_HALO, :] = jnp.where(has_prev, wp_ref[:, cs], 0.0)
        wbuf[c, CONV_B_HALO:CONV_B_HALO + tt, :] = w_ref[:, cs]
        wbuf[c, CONV_B_HALO + tt:, :] = jnp.where(has_next, wn_ref[:, cs], 0.0)

    off_a = CONV_A_HALO - ka // 2
    off_b = CONV_B_HALO - kb // 2

    for c in range(n_lane_tiles):
        cs = slice(c * LANES, (c + 1) * LANES)

        def conv_rows(r, carry, c=c, cs=cs):
            r0 = pl.multiple_of(r * row_chunk, row_chunk)
            bias = jnp.broadcast_to(cab_ref[:, cs], (F32_SUBLANES, LANES))
            accs = [bias] * (row_chunk // F32_SUBLANES)
            for k in range(ka):
                wk = caw_ref[k:k + 1, cs]
                accs = [a + ubuf[c, pl.ds(r0 + (off_a + k + F32_SUBLANES * s), F32_SUBLANES), :] * wk
                        for s, a in enumerate(accs)]
            for s, a in enumerate(accs):
                cbuf[pl.ds(r0 + F32_SUBLANES * s, F32_SUBLANES), cs] = a
            return carry

        lax.fori_loop(0, tt // row_chunk, conv_rows, 0)

    def norm_rows(r, carry):
        r0 = pl.multiple_of(r * ln_chunk, ln_chunk)
        rows = pl.ds(r0, ln_chunk)
        xc = cbuf[rows, :]
        mu = jnp.mean(xc, axis=-1, keepdims=True)
        dev = xc - mu
        var = jnp.mean(dev * dev, axis=-1, keepdims=True)
        y = dev * lax.rsqrt(var + LN_EPS) * ng_ref[...] + nbias_ref[...]
        z_ref[rows, 0:width] = (_silu(y) * ga_ref[rows, :]).astype(BF16)
        for c in range(n_lane_tiles):
            cs = slice(c * LANES, (c + 1) * LANES)
            parts = []
            for s in range(ln_chunk // F32_SUBLANES):
                v = None
                for k in range(kb):
                    term = (wbuf[c, pl.ds(r0 + (off_b + k + F32_SUBLANES * s), F32_SUBLANES), :]
                            * cbw_ref[k:k + 1, cs])
                    v = term if v is None else v + term
                parts.append(v)
            v = jnp.concatenate(parts, axis=0)
            z_ref[rows, width + c * LANES:width + (c + 1) * LANES] = (gb_ref[rows, cs] * v).astype(BF16)
        return carry

    lax.fori_loop(0, tt // ln_chunk, norm_rows, 0, unroll=4)


def _even_conv(u, ga, w, gb, caw, cab, ng, nbias, cbw):
    nb, t, width = u.shape
    ka, kb = caw.shape[0], cbw.shape[0]
    assert ka // 2 <= CONV_A_HALO and kb // 2 <= CONV_B_HALO
    tt = _pick(t, (256, 128))
    n_tiles = t // tt
    ha, hb = tt // CONV_A_HALO, tt // CONV_B_HALO

    def cur(b, i):
        return (b, i, 0)

    def halo_specs(halo, per_tile):
        last = t // halo - 1
        return [pl.BlockSpec((None, halo, width), lambda b, i: (b, jnp.maximum(i * per_tile - 1, 0), 0)),
                pl.BlockSpec((None, halo, width), lambda b, i: (b, jnp.minimum((i + 1) * per_tile, last), 0))]

    tile = pl.BlockSpec((None, tt, width), cur)
    vec = lambda rows: pl.BlockSpec((rows, width), lambda b, i: (0, 0))
    kern = functools.partial(_even_conv_kernel, tt=tt, width=width, n_tiles=n_tiles, ka=ka, kb=kb,
                             row_chunk=64, ln_chunk=BF16_SUBLANES)
    return pl.pallas_call(
        kern,
        grid=(nb, n_tiles),
        in_specs=[tile] + halo_specs(CONV_A_HALO, ha) + [tile] + halo_specs(CONV_B_HALO, hb) + [tile, tile]
                 + [vec(ka), vec(1), vec(1), vec(1), vec(kb)],
        out_specs=pl.BlockSpec((None, tt, 2 * width), cur),
        out_shape=jax.ShapeDtypeStruct((nb, t, 2 * width), BF16),
        scratch_shapes=[pltpu.VMEM((width // LANES, tt + 2 * CONV_A_HALO, LANES), F32),
                        pltpu.VMEM((width // LANES, tt + 2 * CONV_B_HALO, LANES), F32),
                        pltpu.VMEM((tt, width), F32)],
        compiler_params=_cparams("parallel", "parallel"),
        name="even_conv",
    )(u, u, u, w, w, w, ga, gb, caw, cab.reshape(1, width), ng.reshape(1, width), nbias.reshape(1, width), cbw)


def _outproj_ln_kernel(z_ref, w_ref, x_ref, gate_ref, g_ref, b_ref, o_ref, *, alpha):
    tm = x_ref.shape[0]
    rb = _pick(tm, (256, 128))
    for blk in range(tm // rb):
        rows = slice(blk * rb, (blk + 1) * rb)
        y = jnp.dot(z_ref[rows, :], w_ref[...], preferred_element_type=F32)
        r = alpha * x_ref[rows, :] + gate_ref[...] * y
        mu = jnp.mean(r, axis=-1, keepdims=True)
        dev = r - mu
        var = jnp.mean(dev * dev, axis=-1, keepdims=True)
        o_ref[rows, :] = dev * lax.rsqrt(var + LN_EPS) * g_ref[...] + b_ref[...]


def _outproj_ln(z, w_out, li, x, mod5, layer, row_of, ln_g, ln_b, alpha):
    nb, t, d = x.shape
    kz = z.shape[-1]
    tm = _pick(t, (512, 256, 128))
    tile = lambda b, i: (b, i, 0)
    const = lambda b, i: (0, 0)
    return pl.pallas_call(
        functools.partial(_outproj_ln_kernel, alpha=alpha),
        grid=(nb, t // tm),
        in_specs=[pl.BlockSpec((None, tm, kz), tile),
                  pl.BlockSpec((None, kz, d), lambda b, i: (li, 0, 0), pipeline_mode=pl.Buffered(1)),
                  pl.BlockSpec((None, tm, d), tile),
                  _mod_spec(layer, 2, row_of, d),
                  pl.BlockSpec((1, d), const), pl.BlockSpec((1, d), const)],
        out_specs=pl.BlockSpec((None, tm, d), tile),
        out_shape=jax.ShapeDtypeStruct((nb, t, d), F32),
        compiler_params=_cparams("parallel", "parallel"),
        name="outproj_ln",
    )(z, w_out, x, mod5, ln_g.reshape(1, d), ln_b.reshape(1, d))


def _odd_inproj_kernel(*refs, rope, need_q, q_scale):
    x_ref, sh_ref, sc_ref = refs[:3]
    refs = refs[3:]
    if need_q:
        wq_ref, wg_ref = refs[:2]
        refs = refs[2:]
    wk_ref, wv_ref = refs[:2]
    refs = refs[2:]
    if need_q:
        qg_ref = refs[0]
        refs = refs[1:]
    kg_ref = refs[0]
    refs = refs[1:]
    if rope:
        cos_ref, sa_ref, sb_ref = refs[:3]
        refs = refs[3:]
    if need_q:
        q_ref, sg_ref = refs[:2]
        refs = refs[2:]
    k_ref, vt_ref, h_scr = refs

    @pl.when(pl.program_id(2) == 0)
    def _():
        h_scr[...] = (x_ref[...] * (1.0 + sc_ref[...]) + sh_ref[...]).astype(BF16)

    tm = h_scr.shape[0]
    w_kv = jnp.concatenate([wk_ref[...], wv_ref[...]], axis=1)

    def norm_rope(t, g, rows):
        ms = jnp.mean(t * t, axis=-1, keepdims=True)
        t = t * lax.rsqrt(ms + RMS_EPS) * g
        if rope:
            t = (t * cos_ref[rows, :] + pltpu.roll(t, HEAD_DIM - HEAD_DIM // 4, 1) * sa_ref[rows, :]
                 + pltpu.roll(t, HEAD_DIM // 4, 1) * sb_ref[rows, :])
        return t

    rb = _pick(tm, (256, 128))
    for r in range(tm // rb):
        rows = slice(r * rb, (r + 1) * rb)
        h = h_scr[rows, :]
        kv = jnp.dot(h, w_kv, preferred_element_type=F32)
        k_ref[rows, :] = norm_rope(kv[:, :HEAD_DIM], kg_ref[...], rows).astype(BF16)
        v = kv[:, HEAD_DIM:]
        for s in range(rb // LANES):
            vt_ref[r * (rb // LANES) + s] = v[s * LANES:(s + 1) * LANES, :].T.astype(BF16)
        if need_q:
            q = jnp.dot(h, wq_ref[...], preferred_element_type=F32)
            for hh in range(GQA_GROUP):
                hs = slice(hh * HEAD_DIM, (hh + 1) * HEAD_DIM)
                q_ref[hh, :, rows] = (norm_rope(q[:, hs], qg_ref[...], rows) * q_scale).T.astype(BF16)
            sg_ref[rows, :] = _silu(jnp.dot(h, wg_ref[...], preferred_element_type=F32))


def _odd_inproj(x, mod5, layer, row_of, w_in, li, q_gain, k_gain, rope_tabs, need_q, q_scale):
    nb, t, d = x.shape
    kvh = w_in.shape[2] // (2 * GQA_GROUP * HEAD_DIM + 2 * HEAD_DIM)
    dq = kvh * GQA_GROUP * HEAD_DIM
    dkv = kvh * HEAD_DIM
    gw = GQA_GROUP * HEAD_DIM
    tm = _pick(t, (512, 256, 128))
    rope = rope_tabs is not None
    idx = lambda b, i, j: (b, i, 0)
    in_specs = [pl.BlockSpec((None, tm, d), idx), _mod_spec(layer, 0, row_of, d), _mod_spec(layer, 1, row_of, d)]
    args = [x, mod5, mod5]
    if need_q:
        in_specs += [pl.BlockSpec((None, d, gw), lambda b, i, j: (li, 0, j)),
                     pl.BlockSpec((None, d, gw), lambda b, i, j: (li, 0, kvh + j))]
        args += [w_in, w_in]
    koff, voff = 2 * dq // HEAD_DIM, 2 * dq // HEAD_DIM + kvh
    in_specs += [pl.BlockSpec((None, d, HEAD_DIM), lambda b, i, j: (li, 0, koff + j)),
                 pl.BlockSpec((None, d, HEAD_DIM), lambda b, i, j: (li, 0, voff + j))]
    args += [w_in, w_in]
    gain_spec = pl.BlockSpec((1, HEAD_DIM), lambda b, i, j: (0, 0))
    if need_q:
        in_specs.append(gain_spec)
        args.append(q_gain.reshape(1, HEAD_DIM))
    in_specs.append(gain_spec)
    args.append(k_gain.reshape(1, HEAD_DIM))
    if rope:
        in_specs += [pl.BlockSpec((tm, HEAD_DIM), lambda b, i, j: (i, 0))] * 3
        args += list(rope_tabs)
    out_specs, out_shape = [], []
    if need_q:
        out_specs += [pl.BlockSpec((None, GQA_GROUP, HEAD_DIM, tm), lambda b, i, j: (b, j, 0, i)),
                      pl.BlockSpec((None, tm, gw), lambda b, i, j: (b, i, j))]
        out_shape += [jax.ShapeDtypeStruct((nb, kvh * GQA_GROUP, HEAD_DIM, t), BF16),
                      jax.ShapeDtypeStruct((nb, t, dq), F32)]
    out_specs += [pl.BlockSpec((None, tm, HEAD_DIM), lambda b, i, j: (b, i, j)),
                  pl.BlockSpec((None, None, tm // LANES, HEAD_DIM, LANES), lambda b, i, j: (b, j, i, 0, 0))]
    out_shape += [jax.ShapeDtypeStruct((nb, t, dkv), BF16),
                  jax.ShapeDtypeStruct((nb, kvh, t // LANES, HEAD_DIM, LANES), BF16)]
    return pl.pallas_call(
        functools.partial(_odd_inproj_kernel, rope=rope, need_q=need_q, q_scale=q_scale),
        grid=(nb, t // tm, kvh),
        in_specs=in_specs,
        out_specs=out_specs,
        out_shape=out_shape,
        scratch_shapes=[pltpu.VMEM((tm, d), BF16)],
        compiler_params=_cparams("parallel", "parallel", "arbitrary"),
        name="odd_inproj",
    )(*args)


def _flash_kernel(q_ref, k_ref, vt_ref, sg_ref, o_ref, acc, m_scr, l_scr, s_buf, p_buf, a_buf, cmax_buf,
                  *, tq, tk, n_chunks, n_tiles):
    slabs = tk // LANES
    m_scr[...] = jnp.full(m_scr.shape, -jnp.inf, F32)
    l_scr[...] = jnp.zeros(l_scr.shape, F32)
    acc[...] = jnp.zeros(acc.shape, F32)

    def qk(tile, c, par, hh):
        qh = q_ref[hh, :, tile * tq:(tile + 1) * tq]
        kc = k_ref[pl.ds(pl.multiple_of(c * tk, tk), tk), :]
        st = jnp.dot(kc, qh, preferred_element_type=F32)
        s_buf[tile, par, hh] = st
        cmax_buf[tile, par, hh] = jnp.max(st, axis=0, keepdims=True)

    def softmax(tile, par, hh):
        m_old = m_scr[tile, hh]
        m_new = jnp.maximum(m_old, cmax_buf[tile, par, hh])
        p_buf[tile, par, hh] = jnp.exp2(s_buf[tile, par, hh] - m_new).astype(BF16)
        a_buf[tile, par, hh] = jnp.exp2(m_old - m_new)
        m_scr[tile, hh] = m_new

    def pv(tile, c, par, hh):
        vt = jnp.concatenate([vt_ref[c * slabs + s] for s in range(slabs)], axis=1)
        vt = jnp.concatenate([vt, jnp.ones((BF16_SUBLANES, tk), BF16)], axis=0)
        r = jnp.dot(vt, p_buf[tile, par, hh], preferred_element_type=F32)
        alpha = a_buf[tile, par, hh]
        acc[tile, hh] = alpha * acc[tile, hh] + r[:HEAD_DIM]
        l_scr[tile, hh] = alpha * l_scr[tile, hh] + r[HEAD_DIM:HEAD_DIM + 1]

    def time_step(tile, t, par, do_qk=True, do_softmax=True):
        for hh in range(GQA_GROUP):
            if do_qk:
                qk(tile, t + 2, par, hh)
            pv(tile, t, par, hh)
        for hh in range(GQA_GROUP):
            if do_softmax:
                softmax(tile, 1 - par, hh)

    def fill(tile):
        for hh in range(GQA_GROUP):
            qk(tile, 0, 0, hh)
        if n_chunks > 1:
            for hh in range(GQA_GROUP):
                qk(tile, 1, 1, hh)
        for hh in range(GQA_GROUP):
            softmax(tile, 0, hh)

    steps_per_trip = 2
    n_steady = max(n_chunks - 2, 0)
    n_trips = n_steady // steps_per_trip

    def steady(tile):
        def steady_steps(i, carry):
            for j in range(steps_per_trip):
                time_step(tile, steps_per_trip * i + j, j % 2)
            return carry

        lax.fori_loop(0, n_trips, steady_steps, 0)
        for t in range(n_trips * steps_per_trip, n_steady):
            time_step(tile, t, t % 2)

    def drain(tile):
        if n_chunks > 1:
            time_step(tile, n_chunks - 2, (n_chunks - 2) % 2, do_qk=False)
        time_step(tile, n_chunks - 1, (n_chunks - 1) % 2, do_qk=False, do_softmax=False)
        for hh in range(GQA_GROUP):
            hs = slice(hh * HEAD_DIM, (hh + 1) * HEAD_DIM)
            rows = slice(tile * tq, (tile + 1) * tq)
            o = (acc[tile, hh] * (1.0 / l_scr[tile, hh])).T
            o_ref[rows, hs] = (o * sg_ref[rows, hs]).astype(BF16)

    fill(0)
    for tile in range(n_tiles):
        steady(tile)
        drain(tile)
        if tile + 1 < n_tiles:
            fill(tile + 1)


def _flash(q, k, vt, sg):
    nb, t, dq = sg.shape
    n = k.shape[1]
    kvh = vt.shape[1]
    gw = GQA_GROUP * HEAD_DIM
    tq = _pick(t, (256, 128))
    tk = _pick(n, (768, 512, 256, 128))
    n_tiles = 2 if (t // tq) % 2 == 0 else 1
    rows = n_tiles * tq
    ospec = pl.BlockSpec((None, rows, gw), lambda b, g, i: (b, i, g))
    per_head = lambda *tail: (n_tiles, GQA_GROUP) + tail
    per_chunk = lambda *tail: (n_tiles, 2, GQA_GROUP) + tail
    return pl.pallas_call(
        functools.partial(_flash_kernel, tq=tq, tk=tk, n_chunks=n // tk, n_tiles=n_tiles),
        grid=(nb, kvh, t // rows),
        in_specs=[pl.BlockSpec((None, GQA_GROUP, HEAD_DIM, rows), lambda b, g, i: (b, g, 0, i)),
                  pl.BlockSpec((None, n, HEAD_DIM), lambda b, g, i: (b, 0, g)),
                  pl.BlockSpec((None, None, n // LANES, HEAD_DIM, LANES), lambda b, g, i: (b, g, 0, 0, 0)),
                  ospec],
        out_specs=ospec,
        out_shape=jax.ShapeDtypeStruct((nb, t, dq), BF16),
        scratch_shapes=[pltpu.VMEM(per_head(HEAD_DIM, tq), F32),
                        pltpu.VMEM(per_head(1, tq), F32),
                        pltpu.VMEM(per_head(1, tq), F32),
                        pltpu.VMEM(per_chunk(tk, tq), F32),
                        pltpu.VMEM(per_chunk(tk, tq), BF16),
                        pltpu.VMEM(per_chunk(1, tq), F32),
                        pltpu.VMEM(per_chunk(1, tq), F32)],
        compiler_params=_cparams("parallel", "parallel", "parallel"),
        name="flash_gqa",
    )(q, k, vt, sg)


def _rope_tables(n_tokens):
    quarter = HEAD_DIM // 4
    pos = jnp.arange(n_tokens, dtype=jnp.int32)
    row = (pos // GRID_W).astype(F32)
    col = (pos % GRID_W).astype(F32)
    inv_freq = ROPE_THETA ** (-jnp.arange(0, HEAD_DIM // 2, 2, dtype=F32) / (HEAD_DIM // 2))
    ang_r = row[:, None] * inv_freq[None, :]
    ang_c = col[:, None] * inv_freq[None, :]
    cr, sr, cc, sc = jnp.cos(ang_r), jnp.sin(ang_r), jnp.cos(ang_c), jnp.sin(ang_c)
    zero = jnp.zeros((n_tokens, quarter), F32)
    cos = jnp.concatenate([cr, cr, cc, cc], axis=-1)
    sin_a = jnp.concatenate([-sr, zero, -sc, zero], axis=-1)
    sin_b = jnp.concatenate([zero, sr, zero, sc], axis=-1)
    return cos, sin_a, sin_b


@jax.jit
def kernel(x, c, ctx, c_ctx, w_mod, b_mod, post_ln_g, post_ln_b, w_in_e, conv_a_w, conv_a_b, norm_a_g,
           norm_a_b, conv_b_w, w_out_e, w_in_o, q_norm_g, k_norm_g, w_out_o):
    nb, seq, _ = x.shape
    depth = w_mod.shape[0]
    alpha = (2.0 * depth) ** 0.25
    q_scale = HEAD_DIM ** -0.5 * math.log2(math.e)
    lat_row = lambda b: b
    ctx_row = lambda b: nb

    mod5 = _adaln_all(c, c_ctx, w_mod, b_mod)
    w_in_e, w_out_e, w_in_o, w_out_o = (w.astype(BF16) for w in (w_in_e, w_out_e, w_in_o, w_out_o))
    rope_tabs = _rope_tables(seq)

    for layer in range(depth):
        last = layer == depth - 1
        i = layer // 2
        ln_g, ln_b = post_ln_g[layer], post_ln_b[layer]
        if layer % 2 == 0:
            def conv_mix(s, row_of):
                u, ga, w, gb = _even_inproj(s, mod5, layer, row_of, w_in_e, i)
                z = _even_conv(u, ga, w, gb, conv_a_w[i], conv_a_b[i], norm_a_g[i], norm_a_b[i], conv_b_w[i])
                return _outproj_ln(z, w_out_e, i, s, mod5, layer, row_of, ln_g, ln_b, alpha)

            x_new = conv_mix(x, lat_row)
            if not last:
                ctx = conv_mix(ctx, ctx_row)
            x = x_new
        else:
            q_l, sg_l, k_l, vt_l = _odd_inproj(x, mod5, layer, lat_row, w_in_o, i, q_norm_g[i], k_norm_g[i],
                                               rope_tabs, True, q_scale)
            ctx_out = _odd_inproj(ctx, mod5, layer, ctx_row, w_in_o, i, q_norm_g[i], k_norm_g[i],
                                  None, not last, q_scale)
            k_c, vt_c = ctx_out[-2:]
            k_all = jnp.concatenate([k_c, k_l], axis=1)
            vt_all = jnp.concatenate([vt_c, vt_l], axis=2)
            z_l = _flash(q_l, k_all, vt_all, sg_l)
            x = _outproj_ln(z_l, w_out_o, i, x, mod5, layer, lat_row, ln_g, ln_b, alpha)
            if not last:
                q_c, sg_c = ctx_out[:2]
                z_c = _flash(q_c, k_c, vt_c, sg_c)
                ctx = _outproj_ln(z_c, w_out_o, i, ctx, mod5, layer, ctx_row, ln_g, ln_b, alpha)
    return x
```

```python
import functools
import math

import jax
import jax.numpy as jnp
from jax import lax
from jax.experimental import pallas as pl
from jax.experimental.pallas import tpu as pltpu

F32 = jnp.float32
BF16 = jnp.bfloat16

HEAD_DIM = 128
GQA_GROUP = 4
GRID_W = 64
ROPE_THETA = 10000.0
LN_EPS = 1e-5
RMS_EPS = 1e-6

LANES = 128
F32_SUBLANES = 8
BF16_SUBLANES = 16
VMEM_LIMIT_BYTES = 56 * 1024 * 1024

MOD_ROWS = 8
CONV_A_HALO = 16
CONV_B_HALO = 8


def _pick(n, cands):
    for c in cands:
        if n % c == 0:
            return c
    return n


def _sigmoid(x):
    return 1.0 / (1.0 + jnp.exp(-x))


def _silu(x):
    return x * _sigmoid(x)


def _cparams(*sem):
    return pltpu.CompilerParams(dimension_semantics=sem, vmem_limit_bytes=VMEM_LIMIT_BYTES)


def _adaln_kernel(c_ref, w_ref, b_ref, o_ref):
    s = _silu(c_ref[...]).astype(BF16)
    w = w_ref[...].astype(BF16)
    o_ref[...] = jnp.dot(s, w, preferred_element_type=F32) + b_ref[...]


def _adaln_all(c, c_ctx, w_mod, b_mod):
    n_layers, d, d3 = w_mod.shape
    nb = c.shape[0]
    cc = jnp.zeros((MOD_ROWS, d), F32).at[:nb].set(c).at[nb].set(c_ctx)
    tn = _pick(d3, (1024, 512, 256, 128))
    out = pl.pallas_call(
        _adaln_kernel,
        grid=(n_layers, d3 // tn),
        in_specs=[
            pl.BlockSpec((MOD_ROWS, d), lambda l, j: (0, 0)),
            pl.BlockSpec((None, d, tn), lambda l, j: (l, 0, j)),
            pl.BlockSpec((None, 1, tn), lambda l, j: (l, 0, j)),
        ],
        out_specs=pl.BlockSpec((None, MOD_ROWS, tn), lambda l, j: (l, 0, j)),
        out_shape=jax.ShapeDtypeStruct((n_layers, MOD_ROWS, d3), F32),
        compiler_params=_cparams("parallel", "parallel"),
        name="adaln_mod",
    )(cc, w_mod, b_mod.reshape(n_layers, 1, d3))
    return out.reshape(n_layers, MOD_ROWS, 3, d).transpose(0, 2, 1, 3).reshape(n_layers, 3, MOD_ROWS, 1, d)


def _mod_spec(layer, chunk, row_of, d):
    return pl.BlockSpec((None, None, None, 1, d), lambda b, *_: (layer, chunk, row_of(b), 0, 0))


def _even_inproj_kernel(x_ref, sh_ref, sc_ref, w_aval, w_aglu, w_agate, w_bx, w_bb, w_bc, w_bgate,
                        u_ref, ga_ref, w_ref, gb_ref, h_scr):
    @pl.when(pl.program_id(2) == 0)
    def _():
        h_scr[...] = (x_ref[...] * (1.0 + sc_ref[...]) + sh_ref[...]).astype(BF16)

    h = h_scr[...]

    def proj(w):
        return jnp.dot(h, w[...], preferred_element_type=F32)

    u_ref[...] = proj(w_aval) * _sigmoid(proj(w_aglu))
    ga_ref[...] = _silu(proj(w_agate))
    w_ref[...] = proj(w_bc) * proj(w_bx)
    gb_ref[...] = proj(w_bb) * _silu(proj(w_bgate))


def _even_inproj(x, mod5, layer, row_of, w_in, li):
    nb, t, d = x.shape
    width = w_in.shape[2] // 7
    tm = _pick(t, (1024, 512, 256, 128))
    tn = _pick(width, (256, 128))
    ncol = width // tn
    w_specs = [pl.BlockSpec((None, d, tn), lambda b, i, j, c=c: (li, 0, c * ncol + j)) for c in range(7)]
    out_spec = pl.BlockSpec((None, tm, tn), lambda b, i, j: (b, i, j))
    out_sds = jax.ShapeDtypeStruct((nb, t, width), F32)
    return pl.pallas_call(
        _even_inproj_kernel,
        grid=(nb, t // tm, ncol),
        in_specs=[pl.BlockSpec((None, tm, d), lambda b, i, j: (b, i, 0)),
                  _mod_spec(layer, 0, row_of, d), _mod_spec(layer, 1, row_of, d)] + w_specs,
        out_specs=[out_spec] * 4,
        out_shape=[out_sds] * 4,
        scratch_shapes=[pltpu.VMEM((tm, d), BF16)],
        compiler_params=_cparams("parallel", "parallel", "arbitrary"),
        name="even_inproj",
    )(x, mod5, mod5, *([w_in] * 7))


def _even_conv_kernel(u_ref, up_ref, un_ref, w_ref, wp_ref, wn_ref, ga_ref, gb_ref,
                      caw_ref, cab_ref, ng_ref, nbias_ref, cbw_ref, z_ref,
                      ubuf, wbuf, cbuf, *, tt, width, n_tiles, ka, kb, row_chunk, ln_chunk):
    i = pl.program_id(1)
    has_prev = i > 0
    has_next = i < n_tiles - 1
    n_lane_tiles = width // LANES
    for c in range(n_lane_tiles):
        cs = slice(c * LANES, (c + 1) * LANES)
        ubuf[c, 0:CONV_A_HALO, :] = jnp.where(has_prev, up_ref[:, cs], 0.0)
        ubuf[c, CONV_A_HALO:CONV_A_HALO + tt, :] = u_ref[:, cs]
        ubuf[c, CONV_A_HALO + tt:, :] = jnp.where(has_next, un_ref[:, cs], 0.0)
        wbuf[c, 0:CONV_B_HALO, :] = jnp.where(has_prev, wp_ref[:, cs], 0.0)
        wbuf[c, CONV_B_HALO:CONV_B_HALO + tt, :] = w_ref[:, cs]
        wbuf[c, CONV_B_HALO + tt:, :] = jnp.where(has_next, wn_ref[:, cs], 0.0)

    off_a = CONV_A_HALO - ka // 2
    off_b = CONV_B_HALO - kb // 2

    for c in range(n_lane_tiles):
        cs = slice(c * LANES, (c + 1) * LANES)

        def conv_rows(r, carry, c=c, cs=cs):
            r0 = pl.multiple_of(r * row_chunk, row_chunk)
            bias = jnp.broadcast_to(cab_ref[:, cs], (F32_SUBLANES, LANES))
            accs = [bias] * (row_chunk // F32_SUBLANES)
            for k in range(ka):
                wk = caw_ref[k:k + 1, cs]
                accs = [a + ubuf[c, pl.ds(r0 + (off_a + k + F32_SUBLANES * s), F32_SUBLANES), :] * wk
                        for s, a in enumerate(accs)]
            for s, a in enumerate(accs):
                cbuf[pl.ds(r0 + F32_SUBLANES * s, F32_SUBLANES), cs] = a
            return carry

        lax.fori_loop(0, tt // row_chunk, conv_rows, 0)

    def norm_rows(r, carry):
        r0 = pl.multiple_of(r * ln_chunk, ln_chunk)
        rows = pl.ds(r0, ln_chunk)
        xc = cbuf[rows, :]
        mu = jnp.mean(xc, axis=-1, keepdims=True)
        dev = xc - mu
        var = jnp.mean(dev * dev, axis=-1, keepdims=True)
        y = dev * lax.rsqrt(var + LN_EPS) * ng_ref[...] + nbias_ref[...]
        z_ref[rows, 0:width] = (_silu(y) * ga_ref[rows, :]).astype(BF16)
        for c in range(n_lane_tiles):
            cs = slice(c * LANES, (c + 1) * LANES)
            parts = []
            for s in range(ln_chunk // F32_SUBLANES):
                v = None
                for k in range(kb):
                    term = (wbuf[c, pl.ds(r0 + (off_b + k + F32_SUBLANES * s), F32_SUBLANES), :]
                            * cbw_ref[k:k + 1, cs])
                    v = term if v is None else v + term
                parts.append(v)
            v = jnp.concatenate(parts, axis=0)
            z_ref[rows, width + c * LANES:width + (c + 1) * LANES] = (gb_ref[rows, cs] * v).astype(BF16)
        return carry

    lax.fori_loop(0, tt // ln_chunk, norm_rows, 0, unroll=4)


def _even_conv(u, ga, w, gb, caw, cab, ng, nbias, cbw):
    nb, t, width = u.shape
    ka, kb = caw.shape[0], cbw.shape[0]
    assert ka // 2 <= CONV_A_HALO and kb // 2 <= CONV_B_HALO
    tt = _pick(t, (256, 128))
    n_tiles = t // tt
    ha, hb = tt // CONV_A_HALO, tt // CONV_B_HALO

    def cur(b, i):
        return (b, i, 0)

    def halo_specs(halo, per_tile):
        last = t // halo - 1
        return [pl.BlockSpec((None, halo, width), lambda b, i: (b, jnp.maximum(i * per_tile - 1, 0), 0)),
                pl.BlockSpec((None, halo, width), lambda b, i: (b, jnp.minimum((i + 1) * per_tile, last), 0))]

    tile = pl.BlockSpec((None, tt, width), cur)
    vec = lambda rows: pl.BlockSpec((rows, width), lambda b, i: (0, 0))
    kern = functools.partial(_even_conv_kernel, tt=tt, width=width, n_tiles=n_tiles, ka=ka, kb=kb,
                             row_chunk=64, ln_chunk=BF16_SUBLANES)
    return pl.pallas_call(
        kern,
        grid=(nb, n_tiles),
        in_specs=[tile] + halo_specs(CONV_A_HALO, ha) + [tile] + halo_specs(CONV_B_HALO, hb) + [tile, tile]
                 + [vec(ka), vec(1), vec(1), vec(1), vec(kb)],
        out_specs=pl.BlockSpec((None, tt, 2 * width), cur),
        out_shape=jax.ShapeDtypeStruct((nb, t, 2 * width), BF16),
        scratch_shapes=[pltpu.VMEM((width // LANES, tt + 2 * CONV_A_HALO, LANES), F32),
                        pltpu.VMEM((width // LANES, tt + 2 * CONV_B_HALO, LANES), F32),
                        pltpu.VMEM((tt, width), F32)],
        compiler_params=_cparams("parallel", "parallel"),
        name="even_conv",
    )(u, u, u, w, w, w, ga, gb, caw, cab.reshape(1, width), ng.reshape(1, width), nbias.reshape(1, width), cbw)


def _outproj_ln_kernel(z_ref, w_ref, x_ref, gate_ref, g_ref, b_ref, o_ref, *, alpha):
    tm = x_ref.shape[0]
    rb = _pick(tm, (256, 128))
    for blk in range(tm // rb):
        rows = slice(blk * rb, (blk + 1) * rb)
        y = jnp.dot(z_ref[rows, :], w_ref[...], preferred_element_type=F32)
        r = alpha * x_ref[rows, :] + gate_ref[...] * y
        mu = jnp.mean(r, axis=-1, keepdims=True)
        dev = r - mu
        var = jnp.mean(dev * dev, axis=-1, keepdims=True)
        o_ref[rows, :] = dev * lax.rsqrt(var + LN_EPS) * g_ref[...] + b_ref[...]


def _outproj_ln(z, w_out, li, x, mod5, layer, row_of, ln_g, ln_b, alpha):
    nb, t, d = x.shape
    kz = z.shape[-1]
    tm = _pick(t, (512, 256, 128))
    tile = lambda b, i: (b, i, 0)
    const = lambda b, i: (0, 0)
    return pl.pallas_call(
        functools.partial(_outproj_ln_kernel, alpha=alpha),
        grid=(nb, t // tm),
        in_specs=[pl.BlockSpec((None, tm, kz), tile),
                  pl.BlockSpec((None, kz, d), lambda b, i: (li, 0, 0), pipeline_mode=pl.Buffered(1)),
                  pl.BlockSpec((None, tm, d), tile),
                  _mod_spec(layer, 2, row_of, d),
                  pl.BlockSpec((1, d), const), pl.BlockSpec((1, d), const)],
        out_specs=pl.BlockSpec((None, tm, d), tile),
        out_shape=jax.ShapeDtypeStruct((nb, t, d), F32),
        compiler_params=_cparams("parallel", "parallel"),
        name="outproj_ln",
    )(z, w_out, x, mod5, ln_g.reshape(1, d), ln_b.reshape(1, d))


def _odd_inproj_kernel(*refs, rope, need_q, q_scale):
    x_ref, sh_ref, sc_ref = refs[:3]
    refs = refs[3:]
    if need_q:
        wq_ref, wg_ref = refs[:2]
        refs = refs[2:]
    wk_ref, wv_ref = refs[:2]
    refs = refs[2:]
    if need_q:
        qg_ref = refs[0]
        refs = refs[1:]
    kg_ref = refs[0]
    refs = refs[1:]
    if rope:
        cos_ref, sa_ref, sb_ref = refs[:3]
        refs = refs[3:]
    if need_q:
        q_ref, sg_ref = refs[:2]
        refs = refs[2:]
    k_ref, vt_ref, h_scr = refs

    @pl.when(pl.program_id(2) == 0)
    def _():
        h_scr[...] = (x_ref[...] * (1.0 + sc_ref[...]) + sh_ref[...]).astype(BF16)

    tm = h_scr.shape[0]
    w_kv = jnp.concatenate([wk_ref[...], wv_ref[...]], axis=1)

    def norm_rope(t, g, rows):
        ms = jnp.mean(t * t, axis=-1, keepdims=True)
        t = t * lax.rsqrt(ms + RMS_EPS) * g
        if rope:
            t = (t * cos_ref[rows, :] + pltpu.roll(t, HEAD_DIM - HEAD_DIM // 4, 1) * sa_ref[rows, :]
                 + pltpu.roll(t, HEAD_DIM // 4, 1) * sb_ref[rows, :])
        return t

    rb = _pick(tm, (256, 128))
    for r in range(tm // rb):
        rows = slice(r * rb, (r + 1) * rb)
        h = h_scr[rows, :]
        kv = jnp.dot(h, w_kv, preferred_element_type=F32)
        k_ref[rows, :] = norm_rope(kv[:, :HEAD_DIM], kg_ref[...], rows).astype(BF16)
        v = kv[:, HEAD_DIM:]
        for s in range(rb // LANES):
            vt_ref[r * (rb // LANES) + s] = v[s * LANES:(s + 1) * LANES, :].T.astype(BF16)
        if need_q:
            q = jnp.dot(h, wq_ref[...], preferred_element_type=F32)
            for hh in range(GQA_GROUP):
                hs = slice(hh * HEAD_DIM, (hh + 1) * HEAD_DIM)
                q_ref[hh, :, rows] = (norm_rope(q[:, hs], qg_ref[...], rows) * q_scale).T.astype(BF16)
            sg_ref[rows, :] = _silu(jnp.dot(h, wg_ref[...], preferred_element_type=F32))


def _odd_inproj(x, mod5, layer, row_of, w_in, li, q_gain, k_gain, rope_tabs, need_q, q_scale):
    nb, t, d = x.shape
    kvh = w_in.shape[2] // (2 * GQA_GROUP * HEAD_DIM + 2 * HEAD_DIM)
    dq = kvh * GQA_GROUP * HEAD_DIM
    dkv = kvh * HEAD_DIM
    gw = GQA_GROUP * HEAD_DIM
    tm = _pick(t, (512, 256, 128))
    rope = rope_tabs is not None
    idx = lambda b, i, j: (b, i, 0)
    in_specs = [pl.BlockSpec((None, tm, d), idx), _mod_spec(layer, 0, row_of, d), _mod_spec(layer, 1, row_of, d)]
    args = [x, mod5, mod5]
    if need_q:
        in_specs += [pl.BlockSpec((None, d, gw), lambda b, i, j: (li, 0, j)),
                     pl.BlockSpec((None, d, gw), lambda b, i, j: (li, 0, kvh + j))]
        args += [w_in, w_in]
    koff, voff = 2 * dq // HEAD_DIM, 2 * dq // HEAD_DIM + kvh
    in_specs += [pl.BlockSpec((None, d, HEAD_DIM), lambda b, i, j: (li, 0, koff + j)),
                 pl.BlockSpec((None, d, HEAD_DIM), lambda b, i, j: (li, 0, voff + j))]
    args += [w_in, w_in]
    gain_spec = pl.BlockSpec((1, HEAD_DIM), lambda b, i, j: (0, 0))
    if need_q:
        in_specs.append(gain_spec)
        args.append(q_gain.reshape(1, HEAD_DIM))
    in_specs.append(gain_spec)
    args.append(k_gain.reshape(1, HEAD_DIM))
    if rope:
        in_specs += [pl.BlockSpec((tm, HEAD_DIM), lambda b, i, j: (i, 0))] * 3
        args += list(rope_tabs)
    out_specs, out_shape = [], []
    if need_q:
        out_specs += [pl.BlockSpec((None, GQA_GROUP, HEAD_DIM, tm), lambda b, i, j: (b, j, 0, i)),
                      pl.BlockSpec((None, tm, gw), lambda b, i, j: (b, i, j))]
        out_shape += [jax.ShapeDtypeStruct((nb, kvh * GQA_GROUP, HEAD_DIM, t), BF16),
                      jax.ShapeDtypeStruct((nb, t, dq), F32)]
    out_specs += [pl.BlockSpec((None, tm, HEAD_DIM), lambda b, i, j: (b, i, j)),
                  pl.BlockSpec((None, None, tm // LANES, HEAD_DIM, LANES), lambda b, i, j: (b, j, i, 0, 0))]
    out_shape += [jax.ShapeDtypeStruct((nb, t, dkv), BF16),
                  jax.ShapeDtypeStruct((nb, kvh, t // LANES, HEAD_DIM, LANES), BF16)]
    return pl.pallas_call(
        functools.partial(_odd_inproj_kernel, rope=rope, need_q=need_q, q_scale=q_scale),
        grid=(nb, t // tm, kvh),
        in_specs=in_specs,
        out_specs=out_specs,
        out_shape=out_shape,
        scratch_shapes=[pltpu.VMEM((tm, d), BF16)],
        compiler_params=_cparams("parallel", "parallel", "arbitrary"),
        name="odd_inproj",
    )(*args)


def _flash_kernel(q_ref, k_ref, vt_ref, sg_ref, o_ref, acc, m_scr, l_scr, s_buf, p_buf, a_buf, cmax_buf,
                  *, tq, tk, n_chunks, n_tiles):
    slabs = tk // LANES
    m_scr[...] = jnp.full(m_scr.shape, -jnp.inf, F32)
    l_scr[...] = jnp.zeros(l_scr.shape, F32)
    acc[...] = jnp.zeros(acc.shape, F32)

    def qk_all(tile, c, par):
        q_all = jnp.concatenate([q_ref[hh, :, tile * tq:(tile + 1) * tq] for hh in range(GQA_GROUP)], axis=1)
        kc = k_ref[pl.ds(pl.multiple_of(c * tk, tk), tk), :]
        st = jnp.dot(kc, q_all, preferred_element_type=F32)
        for hh in range(GQA_GROUP):
            sh = st[:, hh * tq:(hh + 1) * tq]
            s_buf[tile, par, hh] = sh
            cmax_buf[tile, par, hh] = jnp.max(sh, axis=0, keepdims=True)

    def softmax(tile, par, hh):
        m_old = m_scr[tile, hh]
        m_new = jnp.maximum(m_old, cmax_buf[tile, par, hh])
        p_buf[tile, par, hh] = jnp.exp2(s_buf[tile, par, hh] - m_new).astype(BF16)
        a_buf[tile, par, hh] = jnp.exp2(m_old - m_new)
        m_scr[tile, hh] = m_new

    def pv(tile, c, par, hh):
        vt = jnp.concatenate([vt_ref[c * slabs + s] for s in range(slabs)], axis=1)
        vt = jnp.concatenate([vt, jnp.ones((BF16_SUBLANES, tk), BF16)], axis=0)
        r = jnp.dot(vt, p_buf[tile, par, hh], preferred_element_type=F32)
        alpha = a_buf[tile, par, hh]
        acc[tile, hh] = alpha * acc[tile, hh] + r[:HEAD_DIM]
        l_scr[tile, hh] = alpha * l_scr[tile, hh] + r[HEAD_DIM:HEAD_DIM + 1]

    def time_step(tile, t, par, do_qk=True, do_softmax=True):
        if do_qk:
            qk_all(tile, t + 2, par)
        for hh in range(GQA_GROUP):
            pv(tile, t, par, hh)
        for hh in range(GQA_GROUP):
            if do_softmax:
                softmax(tile, 1 - par, hh)

    def fill(tile):
        qk_all(tile, 0, 0)
        if n_chunks > 1:
            qk_all(tile, 1, 1)
        for hh in range(GQA_GROUP):
            softmax(tile, 0, hh)

    steps_per_trip = 2
    n_steady = max(n_chunks - 2, 0)
    n_trips = n_steady // steps_per_trip

    def steady(tile):
        def steady_steps(i, carry):
            for j in range(steps_per_trip):
                time_step(tile, steps_per_trip * i + j, j % 2)
            return carry

        lax.fori_loop(0, n_trips, steady_steps, 0)
        for t in range(n_trips * steps_per_trip, n_steady):
            time_step(tile, t, t % 2)

    def drain(tile):
        if n_chunks > 1:
            time_step(tile, n_chunks - 2, (n_chunks - 2) % 2, do_qk=False)
        time_step(tile, n_chunks - 1, (n_chunks - 1) % 2, do_qk=False, do_softmax=False)
        for hh in range(GQA_GROUP):
            hs = slice(hh * HEAD_DIM, (hh + 1) * HEAD_DIM)
            rows = slice(tile * tq, (tile + 1) * tq)
            o = (acc[tile, hh] * (1.0 / l_scr[tile, hh])).T
            o_ref[rows, hs] = (o * sg_ref[rows, hs]).astype(BF16)

    fill(0)
    for tile in range(n_tiles):
        steady(tile)
        drain(tile)
        if tile + 1 < n_tiles:
            fill(tile + 1)


def _flash(q, k, vt, sg):
    nb, t, dq = sg.shape
    n = k.shape[1]
    kvh = vt.shape[1]
    gw = GQA_GROUP * HEAD_DIM
    tq = _pick(t, (256, 128))
    tk = _pick(n, (768, 512, 256, 128))
    n_tiles = 2 if (t // tq) % 2 == 0 else 1
    rows = n_tiles * tq
    ospec = pl.BlockSpec((None, rows, gw), lambda b, g, i: (b, i, g))
    per_head = lambda *tail: (n_tiles, GQA_GROUP) + tail
    per_chunk = lambda *tail: (n_tiles, 2, GQA_GROUP) + tail
    return pl.pallas_call(
        functools.partial(_flash_kernel, tq=tq, tk=tk, n_chunks=n // tk, n_tiles=n_tiles),
        grid=(nb, kvh, t // rows),
        in_specs=[pl.BlockSpec((None, GQA_GROUP, HEAD_DIM, rows), lambda b, g, i: (b, g, 0, i)),
                  pl.BlockSpec((None, n, HEAD_DIM), lambda b, g, i: (b, 0, g)),
                  pl.BlockSpec((None, None, n // LANES, HEAD_DIM, LANES), lambda b, g, i: (b, g, 0, 0, 0)),
                  ospec],
        out_specs=ospec,
        out_shape=jax.ShapeDtypeStruct((nb, t, dq), BF16),
        scratch_shapes=[pltpu.VMEM(per_head(HEAD_DIM, tq), F32),
                        pltpu.VMEM(per_head(1, tq), F32),
                        pltpu.VMEM(per_head(1, tq), F32),
                        pltpu.VMEM(per_chunk(tk, tq), F32),
                        pltpu.VMEM(per_chunk(tk, tq), BF16),
                        pltpu.VMEM(per_chunk(1, tq), F32),
                        pltpu.VMEM(per_chunk(1, tq), F32)],
        compiler_params=_cparams("parallel", "parallel", "parallel"),
        name="flash_gqa",
    )(q, k, vt, sg)


def _rope_tables(n_tokens):
    quarter = HEAD_DIM // 4
    pos = jnp.arange(n_tokens, dtype=jnp.int32)
    row = (pos // GRID_W).astype(F32)
    col = (pos % GRID_W).astype(F32)
    inv_freq = ROPE_THETA ** (-jnp.arange(0, HEAD_DIM // 2, 2, dtype=F32) / (HEAD_DIM // 2))
    ang_r = row[:, None] * inv_freq[None, :]
    ang_c = col[:, None] * inv_freq[None, :]
    cr, sr, cc, sc = jnp.cos(ang_r), jnp.sin(ang_r), jnp.cos(ang_c), jnp.sin(ang_c)
    zero = jnp.zeros((n_tokens, quarter), F32)
    cos = jnp.concatenate([cr, cr, cc, cc], axis=-1)
    sin_a = jnp.concatenate([-sr, zero, -sc, zero], axis=-1)
    sin_b = jnp.concatenate([zero, sr, zero, sc], axis=-1)
    return cos, sin_a, sin_b


@jax.jit
def kernel(x, c, ctx, c_ctx, w_mod, b_mod, post_ln_g, post_ln_b, w_in_e, conv_a_w, conv_a_b, norm_a_g,
           norm_a_b, conv_b_w, w_out_e, w_in_o, q_norm_g, k_norm_g, w_out_o):
    nb, seq, _ = x.shape
    depth = w_mod.shape[0]
    alpha = (2.0 * depth) ** 0.25
    q_scale = HEAD_DIM ** -0.5 * math.log2(math.e)
    lat_row = lambda b: b
    ctx_row = lambda b: nb

    mod5 = _adaln_all(c, c_ctx, w_mod, b_mod)
    w_in_e, w_out_e, w_in_o, w_out_o = (w.astype(BF16) for w in (w_in_e, w_out_e, w_in_o, w_out_o))
    rope_tabs = _rope_tables(seq)

    for layer in range(depth):
        last = layer == depth - 1
        i = layer // 2
        ln_g, ln_b = post_ln_g[layer], post_ln_b[layer]
        if layer % 2 == 0:
            def conv_mix(s, row_of):
                u, ga, w, gb = _even_inproj(s, mod5, layer, row_of, w_in_e, i)
                z = _even_conv(u, ga, w, gb, conv_a_w[i], conv_a_b[i], norm_a_g[i], norm_a_b[i], conv_b_w[i])
                return _outproj_ln(z, w_out_e, i, s, mod5, layer, row_of, ln_g, ln_b, alpha)

            x_new = conv_mix(x, lat_row)
            if not last:
                ctx = conv_mix(ctx, ctx_row)
            x = x_new
        else:
            q_l, sg_l, k_l, vt_l = _odd_inproj(x, mod5, layer, lat_row, w_in_o, i, q_norm_g[i], k_norm_g[i],
                                               rope_tabs, True, q_scale)
            ctx_out = _odd_inproj(ctx, mod5, layer, ctx_row, w_in_o, i, q_norm_g[i], k_norm_g[i],
                                  None, not last, q_scale)
            k_c, vt_c = ctx_out[-2:]
            k_all = jnp.concatenate([k_c, k_l], axis=1)
            vt_all = jnp.concatenate([vt_c, vt_l], axis=2)
            z_l = _flash(q_l, k_all, vt_all, sg_l)
            x = _outproj_ln(z_l, w_out_o, i, x, mod5, layer, lat_row, ln_g, ln_b, alpha)
            if not last:
                q_c, sg_c = ctx_out[:2]
                z_c = _flash(q_c, k_c, vt_c, sg_c)
                ctx = _outproj_ln(z_c, w_out_o, i, ctx, mod5, layer, ctx_row, ln_g, ln_b, alpha)
    return x
```

```python
import functools
import math

import jax
import jax.numpy as jnp
from jax import lax
from jax.experimental import pallas as pl
from jax.experimental.pallas import tpu as pltpu

F32 = jnp.float32
BF16 = jnp.bfloat16

HEAD_DIM = 128
GQA_GROUP = 4
GRID_W = 64
ROPE_THETA = 10000.0
LN_EPS = 1e-5
RMS_EPS = 1e-6

LANES = 128
F32_SUBLANES = 8
BF16_SUBLANES = 16
VMEM_LIMIT_BYTES = 56 * 1024 * 1024

MOD_ROWS = 8
CONV_A_HALO = 16
CONV_B_HALO = 8


def _pick(n, cands):
    for c in cands:
        if n % c == 0:
            return c
    return n


def _sigmoid(x):
    return 1.0 / (1.0 + jnp.exp(-x))


def _silu(x):
    return x * _sigmoid(x)


def _cparams(*sem):
    return pltpu.CompilerParams(dimension_semantics=sem, vmem_limit_bytes=VMEM_LIMIT_BYTES)


def _adaln_kernel(c_ref, w_ref, b_ref, o_ref):
    s = _silu(c_ref[...]).astype(BF16)
    w = w_ref[...].astype(BF16)
    o_ref[...] = jnp.dot(s, w, preferred_element_type=F32) + b_ref[...]


def _adaln_all(c, c_ctx, w_mod, b_mod):
    n_layers, d, d3 = w_mod.shape
    nb = c.shape[0]
    cc = jnp.zeros((MOD_ROWS, d), F32).at[:nb].set(c).at[nb].set(c_ctx)
    tn = _pick(d3, (1024, 512, 256, 128))
    out = pl.pallas_call(
        _adaln_kernel,
        grid=(n_layers, d3 // tn),
        in_specs=[
            pl.BlockSpec((MOD_ROWS, d), lambda l, j: (0, 0)),
            pl.BlockSpec((None, d, tn), lambda l, j: (l, 0, j)),
            pl.BlockSpec((None, 1, tn), lambda l, j: (l, 0, j)),
        ],
        out_specs=pl.BlockSpec((None, MOD_ROWS, tn), lambda l, j: (l, 0, j)),
        out_shape=jax.ShapeDtypeStruct((n_layers, MOD_ROWS, d3), F32),
        compiler_params=_cparams("parallel", "parallel"),
        name="adaln_mod",
    )(cc, w_mod, b_mod.reshape(n_layers, 1, d3))
    return out.reshape(n_layers, MOD_ROWS, 3, d).transpose(0, 2, 1, 3).reshape(n_layers, 3, MOD_ROWS, 1, d)


def _mod_spec(layer, chunk, row_of, d):
    return pl.BlockSpec((None, None, None, 1, d), lambda b, *_: (layer, chunk, row_of(b), 0, 0))


def _even_inproj_kernel(x_ref, sh_ref, sc_ref, w_aval, w_aglu, w_agate, w_bx, w_bb, w_bc, w_bgate,
                        u_ref, ga_ref, w_ref, gb_ref, h_scr):
    @pl.when(pl.program_id(2) == 0)
    def _():
        h_scr[...] = (x_ref[...] * (1.0 + sc_ref[...]) + sh_ref[...]).astype(BF16)

    h = h_scr[...]

    def proj(w):
        return jnp.dot(h, w[...], preferred_element_type=F32)

    u_ref[...] = proj(w_aval) * _sigmoid(proj(w_aglu))
    ga_ref[...] = _silu(proj(w_agate))
    w_ref[...] = proj(w_bc) * proj(w_bx)
    gb_ref[...] = proj(w_bb) * _silu(proj(w_bgate))


def _even_inproj(x, mod5, layer, row_of, w_in, li):
    nb, t, d = x.shape
    width = w_in.shape[2] // 7
    tm = _pick(t, (1024, 512, 256, 128))
    tn = _pick(width, (256, 128))
    ncol = width // tn
    w_specs = [pl.BlockSpec((None, d, tn), lambda b, i, j, c=c: (li, 0, c * ncol + j)) for c in range(7)]
    out_spec = pl.BlockSpec((None, tm, tn), lambda b, i, j: (b, i, j))
    out_sds = jax.ShapeDtypeStruct((nb, t, width), F32)
    return pl.pallas_call(
        _even_inproj_kernel,
        grid=(nb, t // tm, ncol),
        in_specs=[pl.BlockSpec((None, tm, d), lambda b, i, j: (b, i, 0)),
                  _mod_spec(layer, 0, row_of, d), _mod_spec(layer, 1, row_of, d)] + w_specs,
        out_specs=[out_spec] * 4,
        out_shape=[out_sds] * 4,
        scratch_shapes=[pltpu.VMEM((tm, d), BF16)],
        compiler_params=_cparams("parallel", "parallel", "arbitrary"),
        name="even_inproj",
    )(x, mod5, mod5, *([w_in] * 7))


def _even_conv_kernel(u_ref, up_ref, un_ref, w_ref, wp_ref, wn_ref, ga_ref, gb_ref,
                      caw_ref, cab_ref, ng_ref, nbias_ref, cbw_ref, z_ref,
                      ubuf, wbuf, cbuf, *, tt, width, n_tiles, ka, kb, row_chunk, ln_chunk):
    i = pl.program_id(1)
    has_prev = i > 0
    has_next = i < n_tiles - 1
    n_lane_tiles = width // LANES
    for c in range(n_lane_tiles):
        cs = slice(c * LANES, (c + 1) * LANES)
        ubuf[c, 0:CONV_A_HALO, :] = jnp.where(has_prev, up_ref[:, cs], 0.0)
        ubuf[c, CONV_A_HALO:CONV_A_HALO + tt, :] = u_ref[:, cs]
        ubuf[c, CONV_A_HALO + tt:, :] = jnp.where(has_next, un_ref[:, cs], 0.0)
        wbuf[c, 0:CONV_B_HALO, :] = jnp.where(has_prev, wp_ref[:, cs], 0.0)
        wbuf[c, CONV_B_HALO:CONV_B_HALO + tt, :] = w_ref[:, cs]
        wbuf[c, CONV_B_HALO + tt:, :] = jnp.where(has_next, wn_ref[:, cs], 0.0)

    off_a = CONV_A_HALO - ka // 2
    off_b = CONV_B_HALO - kb // 2

    for c in range(n_lane_tiles):
        cs = slice(c * LANES, (c + 1) * LANES)

        def conv_rows(r, carry, c=c, cs=cs):
            r0 = pl.multiple_of(r * row_chunk, row_chunk)
            bias = jnp.broadcast_to(cab_ref[:, cs], (F32_SUBLANES, LANES))
            accs = [bias] * (row_chunk // F32_SUBLANES)
            for k in range(ka):
                wk = caw_ref[k:k + 1, cs]
                accs = [a + ubuf[c, pl.ds(r0 + (off_a + k + F32_SUBLANES * s), F32_SUBLANES), :] * wk
                        for s, a in enumerate(accs)]
            for s, a in enumerate(accs):
                cbuf[pl.ds(r0 + F32_SUBLANES * s, F32_SUBLANES), cs] = a
            return carry

        lax.fori_loop(0, tt // row_chunk, conv_rows, 0)

    def norm_rows(r, carry):
        r0 = pl.multiple_of(r * ln_chunk, ln_chunk)
        rows = pl.ds(r0, ln_chunk)
        xc = cbuf[rows, :]
        mu = jnp.mean(xc, axis=-1, keepdims=True)
        dev = xc - mu
        var = jnp.mean(dev * dev, axis=-1, keepdims=True)
        y = dev * lax.rsqrt(var + LN_EPS) * ng_ref[...] + nbias_ref[...]
        z_ref[rows, 0:width] = (_silu(y) * ga_ref[rows, :]).astype(BF16)
        for c in range(n_lane_tiles):
            cs = slice(c * LANES, (c + 1) * LANES)
            parts = []
            for s in range(ln_chunk // F32_SUBLANES):
                v = None
                for k in range(kb):
                    term = (wbuf[c, pl.ds(r0 + (off_b + k + F32_SUBLANES * s), F32_SUBLANES), :]
                            * cbw_ref[k:k + 1, cs])
                    v = term if v is None else v + term
                parts.append(v)
            v = jnp.concatenate(parts, axis=0)
            z_ref[rows, width + c * LANES:width + (c + 1) * LANES] = (gb_ref[rows, cs] * v).astype(BF16)
        return carry

    lax.fori_loop(0, tt // ln_chunk, norm_rows, 0, unroll=4)


def _even_conv(u, ga, w, gb, caw, cab, ng, nbias, cbw):
    nb, t, width = u.shape
    ka, kb = caw.shape[0], cbw.shape[0]
    assert ka // 2 <= CONV_A_HALO and kb // 2 <= CONV_B_HALO
    tt = _pick(t, (256, 128))
    n_tiles = t // tt
    ha, hb = tt // CONV_A_HALO, tt // CONV_B_HALO

    def cur(b, i):
        return (b, i, 0)

    def halo_specs(halo, per_tile):
        last = t // halo - 1
        return [pl.BlockSpec((None, halo, width), lambda b, i: (b, jnp.maximum(i * per_tile - 1, 0), 0)),
                pl.BlockSpec((None, halo, width), lambda b, i: (b, jnp.minimum((i + 1) * per_tile, last), 0))]

    tile = pl.BlockSpec((None, tt, width), cur)
    vec = lambda rows: pl.BlockSpec((rows, width), lambda b, i: (0, 0))
    kern = functools.partial(_even_conv_kernel, tt=tt, width=width, n_tiles=n_tiles, ka=ka, kb=kb,
                             row_chunk=64, ln_chunk=BF16_SUBLANES)
    return pl.pallas_call(
        kern,
        grid=(nb, n_tiles),
        in_specs=[tile] + halo_specs(CONV_A_HALO, ha) + [tile] + halo_specs(CONV_B_HALO, hb) + [tile, tile]
                 + [vec(ka), vec(1), vec(1), vec(1), vec(kb)],
        out_specs=pl.BlockSpec((None, tt, 2 * width), cur),
        out_shape=jax.ShapeDtypeStruct((nb, t, 2 * width), BF16),
        scratch_shapes=[pltpu.VMEM((width // LANES, tt + 2 * CONV_A_HALO, LANES), F32),
                        pltpu.VMEM((width // LANES, tt + 2 * CONV_B_HALO, LANES), F32),
                        pltpu.VMEM((tt, width), F32)],
        compiler_params=_cparams("parallel", "parallel"),
        name="even_conv",
    )(u, u, u, w, w, w, ga, gb, caw, cab.reshape(1, width), ng.reshape(1, width), nbias.reshape(1, width), cbw)


def _outproj_ln_kernel(z_ref, w_ref, x_ref, gate_ref, g_ref, b_ref, o_ref, *, alpha):
    tm = x_ref.shape[0]
    rb = _pick(tm, (256, 128))
    for blk in range(tm // rb):
        rows = slice(blk * rb, (blk + 1) * rb)
        y = jnp.dot(z_ref[rows, :], w_ref[...], preferred_element_type=F32)
        r = alpha * x_ref[rows, :] + gate_ref[...] * y
        mu = jnp.mean(r, axis=-1, keepdims=True)
        dev = r - mu
        var = jnp.mean(dev * dev, axis=-1, keepdims=True)
        o_ref[rows, :] = dev * lax.rsqrt(var + LN_EPS) * g_ref[...] + b_ref[...]


def _outproj_ln(z, w_out, li, x, mod5, layer, row_of, ln_g, ln_b, alpha):
    nb, t, d = x.shape
    kz = z.shape[-1]
    tm = _pick(t, (512, 256, 128))
    tile = lambda b, i: (b, i, 0)
    const = lambda b, i: (0, 0)
    return pl.pallas_call(
        functools.partial(_outproj_ln_kernel, alpha=alpha),
        grid=(nb, t // tm),
        in_specs=[pl.BlockSpec((None, tm, kz), tile),
                  pl.BlockSpec((None, kz, d), lambda b, i: (li, 0, 0), pipeline_mode=pl.Buffered(1)),
                  pl.BlockSpec((None, tm, d), tile),
                  _mod_spec(layer, 2, row_of, d),
                  pl.BlockSpec((1, d), const), pl.BlockSpec((1, d), const)],
        out_specs=pl.BlockSpec((None, tm, d), tile),
        out_shape=jax.ShapeDtypeStruct((nb, t, d), F32),
        compiler_params=_cparams("parallel", "parallel"),
        name="outproj_ln",
    )(z, w_out, x, mod5, ln_g.reshape(1, d), ln_b.reshape(1, d))


def _odd_inproj_kernel(*refs, rope, need_q, q_scale):
    x_ref, sh_ref, sc_ref = refs[:3]
    refs = refs[3:]
    if need_q:
        wq_ref, wg_ref = refs[:2]
        refs = refs[2:]
    wk_ref, wv_ref = refs[:2]
    refs = refs[2:]
    if need_q:
        qg_ref = refs[0]
        refs = refs[1:]
    kg_ref = refs[0]
    refs = refs[1:]
    if rope:
        cos_ref, sa_ref, sb_ref = refs[:3]
        refs = refs[3:]
    if need_q:
        q_ref, sg_ref = refs[:2]
        refs = refs[2:]
    k_ref, vt_ref, h_scr = refs

    @pl.when(pl.program_id(2) == 0)
    def _():
        h_scr[...] = (x_ref[...] * (1.0 + sc_ref[...]) + sh_ref[...]).astype(BF16)

    tm = h_scr.shape[0]
    w_kv = jnp.concatenate([wk_ref[...], wv_ref[...]], axis=1)

    def norm_rope(t, g, rows):
        ms = jnp.mean(t * t, axis=-1, keepdims=True)
        t = t * lax.rsqrt(ms + RMS_EPS) * g
        if rope:
            t = (t * cos_ref[rows, :] + pltpu.roll(t, HEAD_DIM - HEAD_DIM // 4, 1) * sa_ref[rows, :]
                 + pltpu.roll(t, HEAD_DIM // 4, 1) * sb_ref[rows, :])
        return t

    rb = _pick(tm, (256, 128))
    for r in range(tm // rb):
        rows = slice(r * rb, (r + 1) * rb)
        h = h_scr[rows, :]
        kv = jnp.dot(h, w_kv, preferred_element_type=F32)
        k_ref[rows, :] = norm_rope(kv[:, :HEAD_DIM], kg_ref[...], rows).astype(BF16)
        v = kv[:, HEAD_DIM:]
        for s in range(rb // LANES):
            vt_ref[r * (rb // LANES) + s] = v[s * LANES:(s + 1) * LANES, :].T.astype(BF16)
        if need_q:
            q = jnp.dot(h, wq_ref[...], preferred_element_type=F32)
            for hh in range(GQA_GROUP):
                hs = slice(hh * HEAD_DIM, (hh + 1) * HEAD_DIM)
                q_ref[hh, :, rows] = (norm_rope(q[:, hs], qg_ref[...], rows) * q_scale).T.astype(BF16)
            sg_ref[rows, :] = _silu(jnp.dot(h, wg_ref[...], preferred_element_type=F32))


def _odd_inproj(x, mod5, layer, row_of, w_in, li, q_gain, k_gain, rope_tabs, need_q, q_scale):
    nb, t, d = x.shape
    kvh = w_in.shape[2] // (2 * GQA_GROUP * HEAD_DIM + 2 * HEAD_DIM)
    dq = kvh * GQA_GROUP * HEAD_DIM
    dkv = kvh * HEAD_DIM
    gw = GQA_GROUP * HEAD_DIM
    tm = _pick(t, (1024, 512, 256, 128))
    rope = rope_tabs is not None
    idx = lambda b, i, j: (b, i, 0)
    in_specs = [pl.BlockSpec((None, tm, d), idx), _mod_spec(layer, 0, row_of, d), _mod_spec(layer, 1, row_of, d)]
    args = [x, mod5, mod5]
    if need_q:
        in_specs += [pl.BlockSpec((None, d, gw), lambda b, i, j: (li, 0, j)),
                     pl.BlockSpec((None, d, gw), lambda b, i, j: (li, 0, kvh + j))]
        args += [w_in, w_in]
    koff, voff = 2 * dq // HEAD_DIM, 2 * dq // HEAD_DIM + kvh
    in_specs += [pl.BlockSpec((None, d, HEAD_DIM), lambda b, i, j: (li, 0, koff + j)),
                 pl.BlockSpec((None, d, HEAD_DIM), lambda b, i, j: (li, 0, voff + j))]
    args += [w_in, w_in]
    gain_spec = pl.BlockSpec((1, HEAD_DIM), lambda b, i, j: (0, 0))
    if need_q:
        in_specs.append(gain_spec)
        args.append(q_gain.reshape(1, HEAD_DIM))
    in_specs.append(gain_spec)
    args.append(k_gain.reshape(1, HEAD_DIM))
    if rope:
        in_specs += [pl.BlockSpec((tm, HEAD_DIM), lambda b, i, j: (i, 0))] * 3
        args += list(rope_tabs)
    out_specs, out_shape = [], []
    if need_q:
        out_specs += [pl.BlockSpec((None, GQA_GROUP, HEAD_DIM, tm), lambda b, i, j: (b, j, 0, i)),
                      pl.BlockSpec((None, tm, gw), lambda b, i, j: (b, i, j))]
        out_shape += [jax.ShapeDtypeStruct((nb, kvh * GQA_GROUP, HEAD_DIM, t), BF16),
                      jax.ShapeDtypeStruct((nb, t, dq), F32)]
    out_specs += [pl.BlockSpec((None, tm, HEAD_DIM), lambda b, i, j: (b, i, j)),
                  pl.BlockSpec((None, None, tm // LANES, HEAD_DIM, LANES), lambda b, i, j: (b, j, i, 0, 0))]
    out_shape += [jax.ShapeDtypeStruct((nb, t, dkv), BF16),
                  jax.ShapeDtypeStruct((nb, kvh, t // LANES, HEAD_DIM, LANES), BF16)]
    return pl.pallas_call(
        functools.partial(_odd_inproj_kernel, rope=rope, need_q=need_q, q_scale=q_scale),
        grid=(nb, t // tm, kvh),
        in_specs=in_specs,
        out_specs=out_specs,
        out_shape=out_shape,
        scratch_shapes=[pltpu.VMEM((tm, d), BF16)],
        compiler_params=_cparams("parallel", "parallel", "arbitrary"),
        name="odd_inproj",
    )(*args)


def _flash_kernel(q_ref, k_ref, vt_ref, sg_ref, o_ref, acc, m_scr, l_scr, s_buf, p_buf, a_buf, cmax_buf,
                  *, tq, tk, n_chunks, n_tiles):
    slabs = tk // LANES
    m_scr[...] = jnp.full(m_scr.shape, -jnp.inf, F32)
    l_scr[...] = jnp.zeros(l_scr.shape, F32)
    acc[...] = jnp.zeros(acc.shape, F32)

    def qk_all(tile, c, par):
        q_all = jnp.concatenate([q_ref[hh, :, tile * tq:(tile + 1) * tq] for hh in range(GQA_GROUP)], axis=1)
        kc = k_ref[pl.ds(pl.multiple_of(c * tk, tk), tk), :]
        st = jnp.dot(kc, q_all, preferred_element_type=F32)
        for hh in range(GQA_GROUP):
            sh = st[:, hh * tq:(hh + 1) * tq]
            s_buf[tile, par, hh] = sh
            cmax_buf[tile, par, hh] = jnp.max(sh, axis=0, keepdims=True)

    def softmax(tile, par, hh):
        m_old = m_scr[tile, hh]
        m_new = jnp.maximum(m_old, cmax_buf[tile, par, hh])
        p_buf[tile, par, hh] = jnp.exp2(s_buf[tile, par, hh] - m_new).astype(BF16)
        a_buf[tile, par, hh] = jnp.exp2(m_old - m_new)
        m_scr[tile, hh] = m_new

    def pv(tile, c, par, hh):
        vt = jnp.concatenate([vt_ref[c * slabs + s] for s in range(slabs)], axis=1)
        vt = jnp.concatenate([vt, jnp.ones((BF16_SUBLANES, tk), BF16)], axis=0)
        r = jnp.dot(vt, p_buf[tile, par, hh], preferred_element_type=F32)
        alpha = a_buf[tile, par, hh]
        acc[tile, hh] = alpha * acc[tile, hh] + r[:HEAD_DIM]
        l_scr[tile, hh] = alpha * l_scr[tile, hh] + r[HEAD_DIM:HEAD_DIM + 1]

    def time_step(tile, t, par, do_qk=True, do_softmax=True):
        if do_qk:
            qk_all(tile, t + 2, par)
        for hh in range(GQA_GROUP):
            pv(tile, t, par, hh)
        for hh in range(GQA_GROUP):
            if do_softmax:
                softmax(tile, 1 - par, hh)

    def fill(tile):
        qk_all(tile, 0, 0)
        if n_chunks > 1:
            qk_all(tile, 1, 1)
        for hh in range(GQA_GROUP):
            softmax(tile, 0, hh)

    steps_per_trip = 2
    n_steady = max(n_chunks - 2, 0)
    n_trips = n_steady // steps_per_trip

    def steady(tile):
        def steady_steps(i, carry):
            for j in range(steps_per_trip):
                time_step(tile, steps_per_trip * i + j, j % 2)
            return carry

        lax.fori_loop(0, n_trips, steady_steps, 0)
        for t in range(n_trips * steps_per_trip, n_steady):
            time_step(tile, t, t % 2)

    def drain(tile):
        if n_chunks > 1:
            time_step(tile, n_chunks - 2, (n_chunks - 2) % 2, do_qk=False)
        time_step(tile, n_chunks - 1, (n_chunks - 1) % 2, do_qk=False, do_softmax=False)
        for hh in range(GQA_GROUP):
            hs = slice(hh * HEAD_DIM, (hh + 1) * HEAD_DIM)
            rows = slice(tile * tq, (tile + 1) * tq)
            o = (acc[tile, hh] * (1.0 / l_scr[tile, hh])).T
            o_ref[rows, hs] = (o * sg_ref[rows, hs]).astype(BF16)

    fill(0)
    for tile in range(n_tiles):
        steady(tile)
        drain(tile)
        if tile + 1 < n_tiles:
            fill(tile + 1)


def _flash(q, k, vt, sg):
    nb, t, dq = sg.shape
    n = k.shape[1]
    kvh = vt.shape[1]
    gw = GQA_GROUP * HEAD_DIM
    tq = _pick(t, (256, 128))
    tk = _pick(n, (768, 512, 256, 128))
    n_tiles = 2 if (t // tq) % 2 == 0 else 1
    rows = n_tiles * tq
    ospec = pl.BlockSpec((None, rows, gw), lambda b, g, i: (b, i, g))
    per_head = lambda *tail: (n_tiles, GQA_GROUP) + tail
    per_chunk = lambda *tail: (n_tiles, 2, GQA_GROUP) + tail
    return pl.pallas_call(
        functools.partial(_flash_kernel, tq=tq, tk=tk, n_chunks=n // tk, n_tiles=n_tiles),
        grid=(nb, kvh, t // rows),
        in_specs=[pl.BlockSpec((None, GQA_GROUP, HEAD_DIM, rows), lambda b, g, i: (b, g, 0, i)),
                  pl.BlockSpec((None, n, HEAD_DIM), lambda b, g, i: (b, 0, g)),
                  pl.BlockSpec((None, None, n // LANES, HEAD_DIM, LANES), lambda b, g, i: (b, g, 0, 0, 0)),
                  ospec],
        out_specs=ospec,
        out_shape=jax.ShapeDtypeStruct((nb, t, dq), BF16),
        scratch_shapes=[pltpu.VMEM(per_head(HEAD_DIM, tq), F32),
                        pltpu.VMEM(per_head(1, tq), F32),
                        pltpu.VMEM(per_head(1, tq), F32),
                        pltpu.VMEM(per_chunk(tk, tq), F32),
                        pltpu.VMEM(per_chunk(tk, tq), BF16),
                        pltpu.VMEM(per_chunk(1, tq), F32),
                        pltpu.VMEM(per_chunk(1, tq), F32)],
        compiler_params=_cparams("parallel", "parallel", "parallel"),
        name="flash_gqa",
    )(q, k, vt, sg)


def _rope_tables(n_tokens):
    quarter = HEAD_DIM // 4
    pos = jnp.arange(n_tokens, dtype=jnp.int32)
    row = (pos // GRID_W).astype(F32)
    col = (pos % GRID_W).astype(F32)
    inv_freq = ROPE_THETA ** (-jnp.arange(0, HEAD_DIM // 2, 2, dtype=F32) / (HEAD_DIM // 2))
    ang_r = row[:, None] * inv_freq[None, :]
    ang_c = col[:, None] * inv_freq[None, :]
    cr, sr, cc, sc = jnp.cos(ang_r), jnp.sin(ang_r), jnp.cos(ang_c), jnp.sin(ang_c)
    zero = jnp.zeros((n_tokens, quarter), F32)
    cos = jnp.concatenate([cr, cr, cc, cc], axis=-1)
    sin_a = jnp.concatenate([-sr, zero, -sc, zero], axis=-1)
    sin_b = jnp.concatenate([zero, sr, zero, sc], axis=-1)
    return cos, sin_a, sin_b


@jax.jit
def kernel(x, c, ctx, c_ctx, w_mod, b_mod, post_ln_g, post_ln_b, w_in_e, conv_a_w, conv_a_b, norm_a_g,
           norm_a_b, conv_b_w, w_out_e, w_in_o, q_norm_g, k_norm_g, w_out_o):
    nb, seq, _ = x.shape
    depth = w_mod.shape[0]
    alpha = (2.0 * depth) ** 0.25
    q_scale = HEAD_DIM ** -0.5 * math.log2(math.e)
    lat_row = lambda b: b
    ctx_row = lambda b: nb

    mod5 = _adaln_all(c, c_ctx, w_mod, b_mod)
    w_in_e, w_out_e, w_in_o, w_out_o = (w.astype(BF16) for w in (w_in_e, w_out_e, w_in_o, w_out_o))
    rope_tabs = _rope_tables(seq)

    for layer in range(depth):
        last = layer == depth - 1
        i = layer // 2
        ln_g, ln_b = post_ln_g[layer], post_ln_b[layer]
        if layer % 2 == 0:
            def conv_mix(s, row_of):
                u, ga, w, gb = _even_inproj(s, mod5, layer, row_of, w_in_e, i)
                z = _even_conv(u, ga, w, gb, conv_a_w[i], conv_a_b[i], norm_a_g[i], norm_a_b[i], conv_b_w[i])
                return _outproj_ln(z, w_out_e, i, s, mod5, layer, row_of, ln_g, ln_b, alpha)

            x_new = conv_mix(x, lat_row)
            if not last:
                ctx = conv_mix(ctx, ctx_row)
            x = x_new
        else:
            q_l, sg_l, k_l, vt_l = _odd_inproj(x, mod5, layer, lat_row, w_in_o, i, q_norm_g[i], k_norm_g[i],
                                               rope_tabs, True, q_scale)
            ctx_out = _odd_inproj(ctx, mod5, layer, ctx_row, w_in_o, i, q_norm_g[i], k_norm_g[i],
                                  None, not last, q_scale)
            k_c, vt_c = ctx_out[-2:]
            k_all = jnp.concatenate([k_c, k_l], axis=1)
            vt_all = jnp.concatenate([vt_c, vt_l], axis=2)
            z_l = _flash(q_l, k_all, vt_all, sg_l)
            x = _outproj_ln(z_l, w_out_o, i, x, mod5, layer, lat_row, ln_g, ln_b, alpha)
            if not last:
                q_c, sg_c = ctx_out[:2]
                z_c = _flash(q_c, k_c, vt_c, sg_c)
                ctx = _outproj_ln(z_c, w_out_o, i, ctx, mod5, layer, ctx_row, ln_g, ln_b, alpha)
    return x
```
